```python
import math
import jax, jax.numpy as jnp
from jax import lax
import numpy as np

D_MODEL = 1024
BATCH = 16
SEQ = 256
DEPTH = 4
DEC_BATCH = 2
DEC_SEQ = 1024
PAST_LEN = 256

GRID_W = 64
HEAD_DIM = 64
N_MIX_HEADS = D_MODEL // HEAD_DIM
A_HEADS = N_MIX_HEADS // 4
A_DK = HEAD_DIM
A_DV = HEAD_DIM
A_CHUNK = 64
B_QHEADS = N_MIX_HEADS // 2
B_KVHEADS = B_QHEADS // 4
B_GROUP = B_QHEADS // B_KVHEADS
B_HD = HEAD_DIM
WINDOW = 128
C_HEADS = N_MIX_HEADS // 4
C_DH = HEAD_DIM // 2
C_DV = HEAD_DIM
A_WIDTH = A_HEADS * A_DV
B_WIDTH = B_QHEADS * B_HD
C_WIDTH = C_HEADS * C_DV
MIX_WIDTH = A_WIDTH + B_WIDTH + C_WIDTH
IN_SIZES = (A_HEADS * A_DK, A_HEADS * A_DK, A_WIDTH, A_WIDTH, 4 * A_HEADS,
            B_WIDTH, B_KVHEADS * B_HD, B_KVHEADS * B_HD,
            C_HEADS * 2 * C_DH, C_HEADS * 2 * C_DH, C_WIDTH)
IN_COLS = sum(IN_SIZES)
IN_SPLITS = tuple(sum(IN_SIZES[:i + 1]) for i in range(len(IN_SIZES) - 1))
A_GATE_START = IN_SPLITS[3]
QB = 128
ROPE_BASE = 10000.0
EPS = 1e-6
N_KEYS = 128
N_EXPERTS = N_KEYS * N_KEYS
PEER_HEADS = 8
PEER_TOPK = 16
PEER_DQ = 256
FORGET_BIAS = 3.0
F32 = jnp.float32

kernel_name = 'hybrid_mlstm_swa_diffattn_peer_dit_step'


def rms_norm(x, gain):
    xf = x.astype(F32)
    y = xf * lax.rsqrt(jnp.mean(xf * xf, axis=-1, keepdims=True) + EPS)
    return (y * gain.astype(F32)).astype(x.dtype)


def ada_modulation(cond, w, b):
    m = jax.nn.silu(cond) @ w + b
    return jnp.split(m[:, None, :], 6, axis=-1)


def axial_rope_tables(n_tokens, dim):
    quarter = dim // 4
    inv = ROPE_BASE ** (-jnp.arange(quarter, dtype=F32) / quarter)
    t = jnp.arange(n_tokens)
    row = (t // GRID_W).astype(F32)
    col = (t % GRID_W).astype(F32)
    ang = jnp.stack([row[:, None] * inv, col[:, None] * inv], axis=1)
    return jnp.cos(ang), jnp.sin(ang)


def apply_axial_rope(x, cos, sin):
    dim = x.shape[-1]
    half, quarter = dim // 2, dim // 4
    shape = (1, x.shape[1]) + (1,) * (x.ndim - 3) + (quarter,)
    xf = x.astype(F32)

    def rot(h, cs, sn):
        h1, h2 = h[..., :quarter], h[..., quarter:]
        return jnp.concatenate([h1 * cs - h2 * sn, h2 * cs + h1 * sn], axis=-1)

    out = jnp.concatenate([
        rot(xf[..., :half], cos[:, 0].reshape(shape), sin[:, 0].reshape(shape)),
        rot(xf[..., half:], cos[:, 1].reshape(shape), sin[:, 1].reshape(shape))], axis=-1)
    return out.astype(x.dtype)


def mlstm_chunkwise(q, k, v, log_i, log_f, C0, n0, m0):
    B, T, H, dk = q.shape
    L = A_CHUNK
    nc = T // L

    def chunks(a):
        a = a.reshape((B, nc, L, H) + a.shape[3:])
        return jnp.swapaxes(jnp.swapaxes(a, 0, 1), 2, 3)

    tri = jnp.tril(jnp.ones((L, L), dtype=bool))

    def step(carry, xs):
        C, n, m = carry
        qc, kc, vc, ic, fc = xs
        b = jnp.cumsum(fc, axis=-1)
        dmat = jnp.where(tri, b[..., :, None] - b[..., None, :] + ic[..., None, :], -jnp.inf)
        m_row = jnp.maximum(b + m[..., None], dmat.max(axis=-1))
        inter = jnp.exp(b + m[..., None] - m_row)
        s = jnp.einsum('bhjd,bhsd->bhjs', qc, kc) * jnp.exp(dmat - m_row[..., None])
        num = inter[..., None] * jnp.einsum('bhjd,bhdv->bhjv', qc, C) + jnp.einsum('bhjs,bhsv->bhjv', s, vc)
        den = inter * jnp.einsum('bhjd,bhd->bhj', qc, n) + s.sum(axis=-1)
        h = num / jnp.maximum(jnp.abs(den), jnp.exp(-m_row))[..., None]
        g = b[..., -1:] - b + ic
        m_new = jnp.maximum(b[..., -1] + m, g.max(axis=-1))
        decay = jnp.exp(b[..., -1] + m - m_new)
        w = jnp.exp(g - m_new[..., None])
        C_new = decay[..., None, None] * C + jnp.einsum('bhs,bhsd,bhsv->bhdv', w, kc, vc)
        n_new = decay[..., None] * n + jnp.einsum('bhs,bhsd->bhd', w, kc)
        return (C_new, n_new, m_new), h

    xs = (chunks(q), chunks(k), chunks(v), chunks(log_i), chunks(log_f))
    (Cf, nf, mf), hs = lax.scan(step, (C0, n0, m0), xs)
    hs = jnp.swapaxes(jnp.swapaxes(hs, 2, 3), 0, 1).reshape(B, T, H, v.shape[-1])
    return hs, (Cf, nf, mf)


def mlstm_prep(aq, ak, av, ag):
    B, T, _ = aq.shape
    q = aq.reshape(B, T, A_HEADS, A_DK).astype(F32)
    k = ak.reshape(B, T, A_HEADS, A_DK).astype(F32) * (A_DK ** -0.5)
    v = av.reshape(B, T, A_HEADS, A_DV).astype(F32)
    g = ag.reshape(B, T, 4, A_HEADS).astype(F32)
    gates = (g[:, :, 0], jax.nn.log_sigmoid(g[:, :, 1]), g[:, :, 2], jax.nn.log_sigmoid(g[:, :, 3]))
    return q, k, v, gates


def mlstm_bidirectional(q, k, v, gates, state_fwd, state_bwd):
    i_f, lf_f, i_b, lf_b = gates
    h_f, st_f = mlstm_chunkwise(q, k, v, i_f, lf_f, *state_fwd)
    rev = lambda a: jnp.flip(a, axis=1)
    h_b, st_b = mlstm_chunkwise(rev(q), rev(k), rev(v), rev(i_b), rev(lf_b), *state_bwd)
    return h_f + rev(h_b), st_f, st_b


def mlstm_output(h, o, gain):
    B, T = h.shape[:2]
    hn = rms_norm(h, gain.reshape(A_HEADS, A_DV)).reshape(B, T, A_WIDTH)
    return (hn * jax.nn.sigmoid(o.astype(F32))).astype(o.dtype)


def dense_gqa_attention(q, k, v, sink):
    B, T, G, R, d = q.shape
    nb = T // QB
    scale = d ** -0.5
    qb = jnp.swapaxes(q.reshape(B, nb, QB, G, R, d), 0, 1)
    sk = sink.reshape(G, R).astype(F32)

    def blk(qi):
        s = jnp.einsum('bqgrd,bkgd->bgrqk', qi, k).astype(F32) * scale
        s_sink = jnp.broadcast_to(sk[None, :, :, None, None], s.shape[:-1] + (1,))
        p = jax.nn.softmax(jnp.concatenate([s, s_sink], axis=-1), axis=-1)[..., :-1]
        return jnp.einsum('bgrqk,bkgd->bqgrd', p.astype(v.dtype), v)

    out = lax.map(blk, qb)
    return jnp.swapaxes(out, 0, 1).reshape(B, T, G * R * d)


def banded_gqa_attention(q, k, v, k_ctx, v_ctx, sink):
    B, T, G, R, d = q.shape
    nb = T // QB
    scale = d ** -0.5
    qb = q.reshape(B, nb, QB, G, R, d)

    def band(a):
        ap = jnp.pad(a, ((0, 0), (QB, QB), (0, 0), (0, 0))).reshape(B, nb + 2, QB, G, d)
        return jnp.concatenate([ap[:, :-2], ap[:, 1:-1], ap[:, 2:]], axis=2)

    kb, vb = band(k), band(v)
    qpos = jnp.arange(T).reshape(nb, QB)
    kpos = (jnp.arange(nb)[:, None] - 1) * QB + jnp.arange(3 * QB)[None, :]
    valid = ((jnp.abs(qpos[:, :, None] - kpos[:, None, :]) <= WINDOW)
             & (kpos >= 0)[:, None, :] & (kpos < T)[:, None, :])
    s_loc = jnp.einsum('bnqgrd,bnkgd->bngrqk', qb, kb).astype(F32) * scale
    s_loc = jnp.where(valid[None, :, None, None], s_loc, -jnp.inf)
    s_ctx = jnp.einsum('bnqgrd,bpgd->bngrqp', qb, k_ctx).astype(F32) * scale
    s_sink = jnp.broadcast_to(sink.reshape(G, R).astype(F32)[None, None, :, :, None, None], s_loc.shape[:-1] + (1,))
    p = jax.nn.softmax(jnp.concatenate([s_loc, s_ctx, s_sink], axis=-1), axis=-1)
    p_loc = p[..., :3 * QB].astype(v.dtype)
    p_ctx = p[..., 3 * QB:-1].astype(v.dtype)
    out = (jnp.einsum('bngrqk,bnkgd->bnqgrd', p_loc, vb)
           + jnp.einsum('bngrqp,bpgd->bnqgrd', p_ctx, v_ctx))
    return out.reshape(B, T, G * R * d)


def diff_lambda(lam_p, lam_init):
    lp = lam_p.astype(F32)
    return jnp.exp(jnp.sum(lp[0] * lp[1])) - jnp.exp(jnp.sum(lp[2] * lp[3])) + lam_init


def diff_attention(q, k, v, lam):
    B, T = q.shape[:2]
    nb = T // QB
    scale = q.shape[-1] ** -0.5
    qb = jnp.swapaxes(q.reshape((B, nb, QB) + q.shape[2:]), 0, 1)
    vf = v.astype(F32)

    def blk(qi):
        s = jnp.einsum('bqhcd,bkhcd->bhcqk', qi, k).astype(F32) * scale
        p = jax.nn.softmax(s, axis=-1)
        a = p[:, :, 0] - lam * p[:, :, 1]
        return jnp.einsum('bhqk,bkhv->bqhv', a, vf)

    out = lax.map(blk, qb)
    return jnp.swapaxes(out, 0, 1).reshape(B, T, q.shape[2], v.shape[-1])


def diff_output(o, gain, lam_init):
    B, T = o.shape[:2]
    return (rms_norm(o, gain) * (1.0 - lam_init)).reshape(B, T, C_WIDTH)


def peer_ffn(x, wq, subkeys, U, V):
    B, T, D = x.shape
    N = B * T
    xf = x.reshape(N, D)
    q = (xf @ wq).reshape(N, PEER_HEADS, 2, PEER_DQ // 2)
    s = jnp.einsum('nhcd,hckd->nhck', q, subkeys).astype(F32)
    s1, i1 = lax.top_k(s[:, :, 0], PEER_TOPK)
    s2, i2 = lax.top_k(s[:, :, 1], PEER_TOPK)
    cand = (s1[..., :, None] + s2[..., None, :]).reshape(N, PEER_HEADS, PEER_TOPK * PEER_TOPK)
    cidx = (i1[..., :, None] * N_KEYS + i2[..., None, :]).reshape(N, PEER_HEADS, PEER_TOPK * PEER_TOPK)
    top_s, pos = lax.top_k(cand, PEER_TOPK)
    eidx = jnp.take_along_axis(cidx, pos, axis=-1)
    g = jax.nn.softmax(top_s, axis=-1)
    nblk = N // QB
    eidx = eidx.reshape(nblk, QB, PEER_HEADS * PEER_TOPK)
    g = g.reshape(nblk, QB, PEER_HEADS * PEER_TOPK).astype(x.dtype)
    xb = xf.reshape(nblk, QB, D)

    def blk(args):
        xi, ei, gi = args
        act = jnp.einsum('tkd,td->tk', U[ei], xi)
        hw = jax.nn.gelu(act, approximate=False) * gi
        return jnp.einsum('tk,tkd->td', hw, V[ei])

    out = lax.map(blk, (xb, eidx, g))
    return out.reshape(B, T, D)


def mixer_inputs(h, p):
    B, T, _ = h.shape
    aq, ak, av, ao, ag, bq, bk, bv, cq, ck, cv = jnp.split(h @ p['w_in'] + p['b_in'], IN_SPLITS, axis=-1)
    mA = mlstm_prep(aq, ak, av, ag)
    qB = rms_norm(bq.reshape(B, T, B_QHEADS, B_HD), p['b_q_gain'])
    kB = rms_norm(bk.reshape(B, T, B_KVHEADS, B_HD), p['b_k_gain'])
    vB = bv.reshape(B, T, B_KVHEADS, B_HD)
    qC = rms_norm(cq.reshape(B, T, C_HEADS, 2, C_DH), p['c_q_gain'])
    kC = rms_norm(ck.reshape(B, T, C_HEADS, 2, C_DH), p['c_k_gain'])
    vC = cv.reshape(B, T, C_HEADS, C_DV)
    return mA, ao, (qB, kB, vB), (qC, kC, vC)


def layer_tail(x, outs, p, g_a, sh_f, sc_f, g_f):
    mix = jnp.concatenate(outs, axis=-1).astype(x.dtype) @ p['w_out']
    x = x + g_a * mix
    h = rms_norm(x, p['norm_gain'][1]) * (1.0 + sc_f) + sh_f
    return x + g_f * peer_ffn(h, p['peer_wq'], p['peer_subkeys'], p['peer_u'], p['peer_v'])


def context_layer(x, c_ctx, p, lam_init):
    B, T, _ = x.shape
    sh_a, sc_a, g_a, sh_f, sc_f, g_f = ada_modulation(c_ctx[None, :], p['w_ada'], p['b_ada'])
    h = rms_norm(x, p['norm_gain'][0]) * (1.0 + sc_a) + sh_a
    (qA, kA, vA, gates), oA, (qB, kB, vB), (qC, kC, vC) = mixer_inputs(h, p)
    zero = (jnp.zeros((B, A_HEADS, A_DK, A_DV), F32), jnp.zeros((B, A_HEADS, A_DK), F32),
            jnp.zeros((B, A_HEADS), F32))
    hA, st_f, st_b = mlstm_bidirectional(qA, kA, vA, gates, zero, zero)
    out_a = mlstm_output(hA, oA, p['a_out_gain'])
    out_b = dense_gqa_attention(qB.reshape(B, T, B_KVHEADS, B_GROUP, B_HD), kB, vB, p['b_sink'])
    lam = diff_lambda(p['c_lambda'], lam_init)
    out_c = diff_output(diff_attention(qC, kC, vC, lam), p['c_out_gain'], lam_init)
    x = layer_tail(x, (out_a, out_b, out_c), p, g_a, sh_f, sc_f, g_f)
    ctx = (kB, vB, kC.reshape(B, T, C_HEADS, 2 * C_DH), vC,
           jnp.stack([st_f[0], st_b[0]], axis=1), jnp.stack([st_f[1], st_b[1]], axis=1),
           jnp.stack([st_f[2], st_b[2]], axis=1))
    return x, ctx


def latent_layer(x, c, p, lam_init, cache, rope_b, rope_c):
    ck_b, cv_b, ck_c, cv_c, sC, sn, sm = cache
    B, T, _ = x.shape
    P = ck_b.shape[1]
    sh_a, sc_a, g_a, sh_f, sc_f, g_f = ada_modulation(c, p['w_ada'], p['b_ada'])
    h = rms_norm(x, p['norm_gain'][0]) * (1.0 + sc_a) + sh_a
    (qA, kA, vA, gates), oA, (qB, kB, vB), (qC, kC, vC) = mixer_inputs(h, p)
    st_f = (sC[:, 0].astype(F32), sn[:, 0].astype(F32), sm[:, 0].astype(F32))
    st_b = (sC[:, 1].astype(F32), sn[:, 1].astype(F32), sm[:, 1].astype(F32))
    hA, _, _ = mlstm_bidirectional(qA, kA, vA, gates, st_f, st_b)
    out_a = mlstm_output(hA, oA, p['a_out_gain'])
    qB = apply_axial_rope(qB, *rope_b)
    kB = apply_axial_rope(kB, *rope_b)
    out_b = banded_gqa_attention(qB.reshape(B, T, B_KVHEADS, B_GROUP, B_HD), kB, vB, ck_b, cv_b, p['b_sink'])
    qC = apply_axial_rope(qC, *rope_c)
    kC = apply_axial_rope(kC, *rope_c)
    k_all = jnp.concatenate([kC, ck_c.reshape(B, P, C_HEADS, 2, C_DH)], axis=1)
    v_all = jnp.concatenate([vC, cv_c], axis=1)
    lam = diff_lambda(p['c_lambda'], lam_init)
    out_c = diff_output(diff_attention(qC, k_all, v_all, lam), p['c_out_gain'], lam_init)
    return layer_tail(x, (out_a, out_b, out_c), p, g_a, sh_f, sc_f, g_f)


def setup_inputs(seed: int = 0) -> dict:
    key = jax.random.key(seed)
    ks = jax.random.split(key, 32)

    def nrm(k, shape, scale):
        return jax.random.normal(k, shape, F32) * scale

    D = D_MODEL
    forget_cols = np.concatenate([A_GATE_START + A_HEADS + np.arange(A_HEADS),
                                  A_GATE_START + 3 * A_HEADS + np.arange(A_HEADS)])
    b_in = nrm(ks[15], (DEPTH, IN_COLS), 0.02).at[:, forget_cols].add(FORGET_BIAS)
    return {
        'x_prompt': nrm(ks[0], (BATCH, SEQ, D), 1.0),
        'x_sample': nrm(ks[1], (DEC_BATCH, DEC_SEQ, D), 1.0),
        'cache_swa_k': nrm(ks[2], (DEC_BATCH, DEPTH, PAST_LEN, B_KVHEADS, B_HD), 1.0),
        'cache_swa_v': nrm(ks[3], (DEC_BATCH, DEPTH, PAST_LEN, B_KVHEADS, B_HD), 1.0),
        'cache_diff_k': nrm(ks[4], (DEC_BATCH, DEPTH, PAST_LEN, C_HEADS, 2 * C_DH), 1.0),
        'cache_diff_v': nrm(ks[5], (DEC_BATCH, DEPTH, PAST_LEN, C_HEADS, C_DV), 1.0),
        'state_mlstm_C': nrm(ks[6], (DEC_BATCH, DEPTH, 2, A_HEADS, A_DK, A_DV), 0.5),
        'state_mlstm_n': nrm(ks[7], (DEC_BATCH, DEPTH, 2, A_HEADS, A_DK), 0.5),
        'state_mlstm_m': nrm(ks[8], (DEC_BATCH, DEPTH, 2, A_HEADS), 1.0),
        'c': nrm(ks[9], (DEC_BATCH, D), 1.0),
        'c_ctx': nrm(ks[10], (D,), 1.0),
        'w_ada': nrm(ks[11], (DEPTH, D, 6 * D), 0.5 * D ** -0.5),
        'b_ada': nrm(ks[12], (DEPTH, 6 * D), 0.02),
        'norm_gain': 1.0 + nrm(ks[13], (DEPTH, 2, D), 0.02),
        'w_in': nrm(ks[14], (DEPTH, D, IN_COLS), D ** -0.5),
        'b_in': b_in,
        'a_out_gain': 1.0 + nrm(ks[16], (DEPTH, A_WIDTH), 0.02),
        'b_q_gain': 1.0 + nrm(ks[17], (DEPTH, B_HD), 0.02),
        'b_k_gain': 1.0 + nrm(ks[18], (DEPTH, B_HD), 0.02),
        'b_sink': nrm(ks[19], (DEPTH, B_QHEADS), 0.5),
        'c_q_gain': 1.0 + nrm(ks[20], (DEPTH, 2, C_DH), 0.02),
        'c_k_gain': 1.0 + nrm(ks[21], (DEPTH, 2, C_DH), 0.02),
        'c_lambda': nrm(ks[22], (DEPTH, 4, C_DH), 0.1),
        'c_out_gain': 1.0 + nrm(ks[23], (DEPTH, C_DV), 0.02),
        'w_out': nrm(ks[24], (DEPTH, MIX_WIDTH, D), MIX_WIDTH ** -0.5),
        'peer_wq': nrm(ks[25], (DEPTH, D, PEER_HEADS * PEER_DQ), D ** -0.5),
        'peer_subkeys': nrm(ks[26], (DEPTH, PEER_HEADS, 2, N_KEYS, PEER_DQ // 2), (PEER_DQ // 2) ** -0.5),
        'peer_u': nrm(ks[27], (DEPTH, N_EXPERTS, D), D ** -0.5),
        'peer_v': nrm(ks[28], (DEPTH, N_EXPERTS, D), 0.5),
    }


def reference(x_prompt, x_sample, cache_swa_k, cache_swa_v, cache_diff_k, cache_diff_v,
              state_mlstm_C, state_mlstm_n, state_mlstm_m, c, c_ctx, w_ada, b_ada, norm_gain,
              w_in, b_in, a_out_gain, b_q_gain, b_k_gain, b_sink, c_q_gain, c_k_gain, c_lambda,
              c_out_gain, w_out, peer_wq, peer_subkeys, peer_u, peer_v):
    rows = x_sample.shape[1] // GRID_W
    n_lat = rows * GRID_W
    rope_b = axial_rope_tables(n_lat, B_HD)
    rope_c = axial_rope_tables(n_lat, C_DH)
    yp, ys = x_prompt, x_sample
    ctx_layers = []
    for l in range(DEPTH):
        p = {'w_ada': w_ada[l], 'b_ada': b_ada[l], 'norm_gain': norm_gain[l], 'w_in': w_in[l],
             'b_in': b_in[l], 'a_out_gain': a_out_gain[l], 'b_q_gain': b_q_gain[l],
             'b_k_gain': b_k_gain[l], 'b_sink': b_sink[l], 'c_q_gain': c_q_gain[l],
             'c_k_gain': c_k_gain[l], 'c_lambda': c_lambda[l], 'c_out_gain': c_out_gain[l],
             'w_out': w_out[l], 'peer_wq': peer_wq[l], 'peer_subkeys': peer_subkeys[l],
             'peer_u': peer_u[l], 'peer_v': peer_v[l]}
        lam_init = 0.8 - 0.6 * math.exp(-0.3 * l)
        yp, ctx = context_layer(yp, c_ctx, p, lam_init)
        ctx_layers.append(ctx)
        cache_l = (cache_swa_k[:, l], cache_swa_v[:, l], cache_diff_k[:, l], cache_diff_v[:, l],
                   state_mlstm_C[:, l], state_mlstm_n[:, l], state_mlstm_m[:, l])
        ys = latent_layer(ys, c, p, lam_init, cache_l, rope_b, rope_c)
    stack = lambda j: jnp.stack([ctx[j] for ctx in ctx_layers], axis=1)
    new_swa_k, new_swa_v = stack(0), stack(1)
    new_diff_k, new_diff_v = stack(2), stack(3)
    new_C, new_n, new_m = stack(4), stack(5), stack(6)
    return (yp, ys, new_swa_k, new_swa_v, new_diff_k, new_diff_v, new_C, new_n, new_m)
```

```python
import functools
import math

import jax
import jax.numpy as jnp
from jax import lax
from jax.experimental import pallas as pl
from jax.experimental.pallas import tpu as pltpu

F32 = jnp.float32
BF16 = jnp.bfloat16
HIGHEST = lax.Precision.HIGHEST
SDS = jax.ShapeDtypeStruct

D_MODEL = 1024
BATCH = 16
SEQ = 256
DEPTH = 4
DEC_BATCH = 2
DEC_SEQ = 1024
PAST_LEN = 256
GRID_W = 64
WINDOW = 128
ROPE_BASE = 10000.0
EPS = 1e-6
N_KEYS = 128
N_EXPERTS = N_KEYS * N_KEYS
PEER_HEADS = 8
PEER_TOPK = 16
A_DK = 64
B_HD = 64
C_DH = 32

LANES = 128
N_CTX = BATCH * SEQ
N_LAT = DEC_BATCH * DEC_SEQ
N_TOK = N_CTX + N_LAT

COL_AQ, COL_AK, COL_AV, COL_AO = 0, 256, 512, 768
COL_BQ, COL_BK, COL_BV = 1024, 1536, 1664
COL_CQ, COL_CK, COL_CV = 1792, 2048, 2304
COL_AG = 2560
Z_COLS = 2688
GATE_SRC = 1024
N_GATES = 16

PV_BQ, PV_BK, PV_CQ, PV_CK, PV_AO, PV_CO, PV_SINK, PV_LAM = 0, 1, 2, 3, 4, 6, 7, 8
PV_ROWS = 16

TM_TOK = 512
TQ_LAT = 256
TL_TOPK = 256
NT_PEER = 512
EB_PEER = 512
VMEM_LIMIT = 52 * 1024 * 1024


def _iota(shape, dim):
    return lax.broadcasted_iota(jnp.int32, shape, dim)


def _dot_nt(a, b):
    return lax.dot_general(a, b, (((1,), (1,)), ((), ())), preferred_element_type=F32)


def _dot(a, b):
    return jnp.dot(a, b, preferred_element_type=F32)


def _group_of_tile(i, tm):
    n_ctx_tiles = N_CTX // tm
    per_batch = DEC_SEQ // tm
    return jnp.where(i < n_ctx_tiles, 0, 1 + (i - n_ctx_tiles) // per_batch)


def _norm_mod(x, gain, scale, shift):
    y = x * lax.rsqrt(jnp.mean(x * x, axis=-1, keepdims=True) + EPS)
    return (y * gain) * (1.0 + scale) + shift


def _ada_body(c_ref, w_ref, b_ref, o_ref):
    c = c_ref[...]
    a = c * jax.nn.sigmoid(c)
    o_ref[...] = jnp.dot(a, w_ref[...], precision=HIGHEST, preferred_element_type=F32) + b_ref[...]


def _ada_all(cond8, w_ada, b_ada):
    tn = 1536
    return pl.pallas_call(
        _ada_body,
        grid=(DEPTH, 6 * D_MODEL // tn),
        in_specs=[pl.BlockSpec((8, D_MODEL), lambda l, j: (0, 0)),
                  pl.BlockSpec((None, D_MODEL, tn), lambda l, j: (l, 0, j)),
                  pl.BlockSpec((None, 1, tn), lambda l, j: (l, 0, j))],
        out_specs=pl.BlockSpec((None, 8, tn), lambda l, j: (l, 0, j)),
        out_shape=SDS((DEPTH, 8, 6 * D_MODEL), F32),
        compiler_params=pltpu.CompilerParams(vmem_limit_bytes=VMEM_LIMIT),
        name="ada_mod",
    )(cond8, w_ada, b_ada.reshape(DEPTH, 1, 6 * D_MODEL))


def _in_body(x_ref, mod_ref, g_ref, w_ref, b_ref, z_ref):
    h = _norm_mod(x_ref[...], g_ref[...], mod_ref[1:2, :], mod_ref[0:1, :])
    z_ref[...] = _dot(h.astype(BF16), w_ref[...]) + b_ref[...]


def _in_proj(x, mod, gain, w, b):
    tm = TM_TOK
    return pl.pallas_call(
        _in_body,
        grid=(N_TOK // tm,),
        in_specs=[pl.BlockSpec((tm, D_MODEL), lambda i: (i, 0)),
                  pl.BlockSpec((None, 6, D_MODEL), lambda i: (_group_of_tile(i, tm), 0, 0)),
                  pl.BlockSpec((1, D_MODEL), lambda i: (0, 0)),
                  pl.BlockSpec((D_MODEL, Z_COLS), lambda i: (0, 0)),
                  pl.BlockSpec((1, Z_COLS), lambda i: (0, 0))],
        out_specs=pl.BlockSpec((tm, Z_COLS), lambda i: (i, 0)),
        out_shape=SDS((N_TOK, Z_COLS), F32),
        compiler_params=pltpu.CompilerParams(vmem_limit_bytes=VMEM_LIMIT),
        name="in_proj",
    )(x, mod, gain, w, b)


def _block_mean_sq(x, blk):
    r = _iota((LANES, LANES), 0) // blk
    c = _iota((LANES, LANES), 1) // blk
    ones = jnp.where(r == c, 1.0 / blk, 0.0).astype(F32)
    return jnp.dot(x * x, ones, precision=HIGHEST, preferred_element_type=F32)


def _rms(x, gain, blk):
    return x * lax.rsqrt(_block_mean_sq(x, blk) + EPS) * gain


def _rope(x, cos, sin_signed, quarter):
    lane = _iota(x.shape, 1)
    first = (lane % (2 * quarter)) < quarter
    partner = jnp.where(first, pltpu.roll(x, LANES - quarter, axis=1), pltpu.roll(x, quarter, axis=1))
    return x * cos + partner * sin_signed


def _dup_half(x, g):
    lane = _iota(x.shape, 1)
    r = pltpu.roll(x, LANES // 2, axis=1)
    return jnp.where(lane < LANES // 2, x, r) if g == 0 else jnp.where(lane < LANES // 2, r, x)


def _lane_block_mask(shape, start, width):
    lane = _iota(shape, 1)
    return (lane >= start) & (lane < start + width)


def _mixer_body(*refs, latent, tq, T, lam_init):
    if latent:
        (z_ref, pv_ref, rope_ref, ckb_ref, cvb_ref, ckc_ref, cvc_ref, cbd_ref, n0_ref, m0_ref,
         mix_ref, bsel_sc, bselT_sc, zgT_sc, kb_sc, kc_sc) = refs
    else:
        (z_ref, pv_ref, mix_ref, kb_ref, vb_ref, kc_ref, vc_ref, cst_ref, nst_ref, mst_ref,
         bsel_sc, bselT_sc, zgT_sc, kb_sc, kc_sc) = refs

    qi = pl.program_id(1)
    r0 = pl.multiple_of(qi * tq, tq)
    rows = pl.ds(r0, tq)
    half = LANES // 2

    @pl.when(qi == 0)
    def _():
        zg = z_ref[:, COL_AG:COL_AG + LANES]
        ls = jax.nn.log_sigmoid(zg)
        rr = _iota((T, T), 0)
        cc = _iota((T, T), 1)
        tril = (rr >= cc).astype(F32)
        triu = (rr <= cc).astype(F32)
        bf = jnp.dot(tril, ls, precision=HIGHEST, preferred_element_type=F32)
        bb = jnp.dot(triu, ls, precision=HIGHEST, preferred_element_type=F32)
        bsel = jnp.where(_iota((T, LANES), 1) < 8, bf, bb)
        bsel_sc[...] = bsel
        bselT_sc[...] = bsel.T
        zgT_sc[...] = zg.T
        kb = _rms(z_ref[:, COL_BK:COL_BK + LANES], pv_ref[PV_BK:PV_BK + 1, :], B_HD)
        if latent:
            kb = _rope(kb, rope_ref[0], rope_ref[1], B_HD // 4)
        kb_sc[...] = kb
        for p in range(2):
            kc = _rms(z_ref[:, COL_CK + LANES * p:COL_CK + LANES * (p + 1)], pv_ref[PV_CK:PV_CK + 1, :], C_DH)
            if latent:
                kc = _rope(kc, rope_ref[2], rope_ref[3], C_DH // 4)
            kc_sc[p] = kc

    lane_q = _iota((tq, LANES), 1)
    lo_q = lane_q < half
    jq = r0 + _iota((tq, T), 0)
    sk = _iota((tq, T), 1)

    bq = bsel_sc[rows, :]
    if latent:
        bq_m0 = bq + m0_ref[0:1, :]
    else:
        bq_m0 = bq
    for p in range(2):
        q_pair = z_ref[rows, COL_AQ + LANES * p:COL_AQ + LANES * (p + 1)]
        k_pair = (z_ref[:, COL_AK + LANES * p:COL_AK + LANES * (p + 1)] * (A_DK ** -0.5)).astype(BF16)
        v_pair = z_ref[:, COL_AV + LANES * p:COL_AV + LANES * (p + 1)].astype(BF16)
        o_pair = z_ref[rows, COL_AO + LANES * p:COL_AO + LANES * (p + 1)]
        outs = []
        for hh in range(2):
            h = 2 * p + hh
            hmask = lo_q if hh == 0 else jnp.logical_not(lo_q)
            q_h = jnp.where(hmask, q_pair, 0.0)
            qk = _dot_nt(q_h.astype(BF16), k_pair)
            prob = jnp.zeros((tq, T), F32)
            extra = jnp.zeros((tq, LANES), F32)
            for d in range(2):
                c = 4 + 8 * d + h
                bcol = bq[:, c:c + 1]
                brow = bselT_sc[c:c + 1, :]
                irow = zgT_sc[c - 4:c - 3, :]
                causal = (sk <= jq) if d == 0 else (sk >= jq)
                dm = jnp.where(causal, bcol - brow + irow, -jnp.inf)
                bm = bq_m0[:, c:c + 1]
                mrow = jnp.maximum(bm, jnp.max(dm, axis=1, keepdims=True))
                s = qk * jnp.exp(dm - mrow)
                den = jnp.sum(s, axis=1, keepdims=True)
                if latent:
                    inter = jnp.exp(bm - mrow)
                    qn0 = jnp.sum(q_h * n0_ref[2 * d + p:2 * d + p + 1, :], axis=1, keepdims=True)
                    den = den + inter * qn0
                rdn = 1.0 / jnp.maximum(jnp.abs(den), jnp.exp(-mrow))
                prob = prob + s * rdn
                if latent:
                    qc0 = _dot(q_h.astype(BF16), cbd_ref[d, p].astype(BF16))
                    extra = extra + (inter * rdn) * qc0
            outs.append(_dot(prob.astype(BF16), v_pair) + extra)
        hcat = jnp.where(lo_q, outs[0], outs[1])
        hn = _rms(hcat, pv_ref[PV_AO + p:PV_AO + p + 1, :], half)
        mix_ref[:, LANES * p:LANES * (p + 1)] = hn * jax.nn.sigmoid(o_pair)

    if not latent:
        zg = z_ref[:, COL_AG:COL_AG + LANES]
        bsel = bsel_sc[...]
        lane_t = _iota((T, LANES), 1)
        tot = jnp.where(lane_t[0:1, :] < 8, bsel[T - 1:T, :], bsel[0:1, :])
        g = tot - bsel + pltpu.roll(zg, 4, axis=1)
        mfin = jnp.maximum(tot, jnp.max(g, axis=0, keepdims=True))
        w = jnp.exp(g - mfin)
        mst_ref[...] = mfin
        lo_t = lane_t < half
        for d in range(2):
            for p in range(2):
                c0 = 4 + 8 * d + 2 * p
                wsel = jnp.where(lo_t, w[:, c0:c0 + 1], w[:, c0 + 1:c0 + 2])
                k_pair = z_ref[:, COL_AK + LANES * p:COL_AK + LANES * (p + 1)] * (A_DK ** -0.5)
                v_pair = z_ref[:, COL_AV + LANES * p:COL_AV + LANES * (p + 1)]
                kw = k_pair * wsel
                cst_ref[d, p] = lax.dot_general(kw.astype(BF16), v_pair.astype(BF16),
                                                (((0,), (0,)), ((), ())), preferred_element_type=F32)
                nst_ref[2 * d + p:2 * d + p + 1, :] = jnp.sum(kw, axis=0, keepdims=True)

    kb = kb_sc[...]
    vb = z_ref[:, COL_BV:COL_BV + LANES]
    if not latent:
        kb_ref[...] = kb
        vb_ref[...] = vb
    scale_b = B_HD ** -0.5
    for g in range(2):
        kd = _dup_half(kb, g).astype(BF16)
        vd = _dup_half(vb, g).astype(BF16)
        if latent:
            ckd = _dup_half(ckb_ref[...], g).astype(BF16)
            cvd = _dup_half(cvb_ref[...], g).astype(BF16)
            in_window = jnp.abs(jq - sk) <= WINDOW
        for pp in range(2):
            p = 2 * g + pp
            qn = _rms(z_ref[rows, COL_BQ + LANES * p:COL_BQ + LANES * (p + 1)], pv_ref[PV_BQ:PV_BQ + 1, :], B_HD)
            if latent:
                qn = _rope(qn, rope_ref[0, rows, :], rope_ref[1, rows, :], B_HD // 4)
            outs = []
            for hh in range(2):
                hmask = lo_q if hh == 0 else jnp.logical_not(lo_q)
                q_h = jnp.where(hmask, qn, 0.0).astype(BF16)
                sink = pv_ref[PV_SINK:PV_SINK + 1, 2 * p + hh:2 * p + hh + 1]
                s = _dot_nt(q_h, kd) * scale_b
                if latent:
                    s = jnp.where(in_window, s, -jnp.inf)
                    sc = _dot_nt(q_h, ckd) * scale_b
                    m = jnp.maximum(jnp.max(s, axis=1, keepdims=True), jnp.max(sc, axis=1, keepdims=True))
                else:
                    m = jnp.max(s, axis=1, keepdims=True)
                m = jnp.maximum(m, sink)
                e = jnp.exp(s - m)
                l = jnp.sum(e, axis=1, keepdims=True) + jnp.exp(sink - m)
                o = _dot(e.astype(BF16), vd)
                if latent:
                    ec = jnp.exp(sc - m)
                    l = l + jnp.sum(ec, axis=1, keepdims=True)
                    o = o + _dot(ec.astype(BF16), cvd)
                outs.append(o * (1.0 / l))
            mix_ref[:, 256 + LANES * p:256 + LANES * (p + 1)] = jnp.where(lo_q, outs[0], outs[1])

    lp = pv_ref[PV_LAM:PV_LAM + 4, :]
    lam = (jnp.exp(jnp.sum(lp[0:1] * lp[1:2], axis=1, keepdims=True))
           - jnp.exp(jnp.sum(lp[2:3] * lp[3:4], axis=1, keepdims=True)) + lam_init)
    scale_c = C_DH ** -0.5
    for p in range(2):
        kc = kc_sc[p]
        vc = z_ref[:, COL_CV + LANES * p:COL_CV + LANES * (p + 1)]
        if not latent:
            kc_ref[:, LANES * p:LANES * (p + 1)] = kc
            vc_ref[:, LANES * p:LANES * (p + 1)] = vc
        kcb = kc.astype(BF16)
        vcb = vc.astype(BF16)
        if latent:
            ckc = ckc_ref[:, LANES * p:LANES * (p + 1)].astype(BF16)
            cvc = cvc_ref[:, LANES * p:LANES * (p + 1)].astype(BF16)
        qn = _rms(z_ref[rows, COL_CQ + LANES * p:COL_CQ + LANES * (p + 1)], pv_ref[PV_CQ:PV_CQ + 1, :], C_DH)
        if latent:
            qn = _rope(qn, rope_ref[2, rows, :], rope_ref[3, rows, :], C_DH // 4)
        outs = []
        for hh in range(2):
            a_loc = None
            a_ctx = None
            for c in range(2):
                q_m = jnp.where(_lane_block_mask((tq, LANES), half * hh + C_DH * c, C_DH), qn, 0.0).astype(BF16)
                s = _dot_nt(q_m, kcb) * scale_c
                m = jnp.max(s, axis=1, keepdims=True)
                if latent:
                    sc = _dot_nt(q_m, ckc) * scale_c
                    m = jnp.maximum(m, jnp.max(sc, axis=1, keepdims=True))
                e = jnp.exp(s - m)
                l = jnp.sum(e, axis=1, keepdims=True)
                if latent:
                    ec = jnp.exp(sc - m)
                    l = l + jnp.sum(ec, axis=1, keepdims=True)
                rl = 1.0 / l
                if c == 0:
                    a_loc = e * rl
                    if latent:
                        a_ctx = ec * rl
                else:
                    a_loc = a_loc - lam * (e * rl)
                    if latent:
                        a_ctx = a_ctx - lam * (ec * rl)
            o = _dot(a_loc.astype(BF16), vcb)
            if latent:
                o = o + _dot(a_ctx.astype(BF16), cvc)
            outs.append(o)
        ocat = jnp.where(lo_q, outs[0], outs[1])
        mix_ref[:, 768 + LANES * p:768 + LANES * (p + 1)] = (
            _rms(ocat, pv_ref[PV_CO:PV_CO + 1, :], half) * (1.0 - lam_init))


def _mixer_scratch(T):
    return [pltpu.VMEM((T, LANES), F32), pltpu.VMEM((LANES, T), F32), pltpu.VMEM((LANES, T), F32),
            pltpu.VMEM((T, LANES), F32), pltpu.VMEM((2, T, LANES), F32)]


def _mixer_ctx(z, pv, lam_init):
    T = SEQ
    body = functools.partial(_mixer_body, latent=False, tq=T, T=T, lam_init=lam_init)
    per_b = lambda shape: pl.BlockSpec((None,) + shape, lambda b, qi: (b,) + (0,) * len(shape))
    return pl.pallas_call(
        body,
        grid=(BATCH, 1),
        in_specs=[pl.BlockSpec((T, Z_COLS), lambda b, qi: (b, 0)),
                  pl.BlockSpec((PV_ROWS, LANES), lambda b, qi: (0, 0))],
        out_specs=[pl.BlockSpec((T, D_MODEL), lambda b, qi: (b, 0)),
                   per_b((T, LANES)), per_b((T, LANES)), per_b((T, 2 * LANES)), per_b((T, 2 * LANES)),
                   per_b((2, 2, LANES, LANES)), per_b((4, LANES)), per_b((1, LANES))],
        out_shape=[SDS((N_CTX, D_MODEL), F32),
                   SDS((BATCH, T, LANES), F32), SDS((BATCH, T, LANES), F32),
                   SDS((BATCH, T, 2 * LANES), F32), SDS((BATCH, T, 2 * LANES), F32),
                   SDS((BATCH, 2, 2, LANES, LANES), F32), SDS((BATCH, 4, LANES), F32), SDS((BATCH, 1, LANES), F32)],
        scratch_shapes=_mixer_scratch(T),
        compiler_params=pltpu.CompilerParams(vmem_limit_bytes=VMEM_LIMIT,
                                             dimension_semantics=("arbitrary", "arbitrary")),
        name="mixer_ctx",
    )(z, pv)


def _mixer_lat(z, pv, rope, ckb, cvb, ckc, cvc, cbd, n0, m0, layer, lam_init):
    T = DEC_SEQ
    tq = TQ_LAT
    P = PAST_LEN
    body = functools.partial(_mixer_body, latent=True, tq=tq, T=T, lam_init=lam_init)
    ctx_off = N_CTX // T
    cache = lambda shape: pl.BlockSpec((None, None) + shape, lambda b, qi: (b, layer) + (0,) * len(shape))
    return pl.pallas_call(
        body,
        grid=(DEC_BATCH, T // tq),
        in_specs=[pl.BlockSpec((T, Z_COLS), lambda b, qi: (ctx_off + b, 0)),
                  pl.BlockSpec((PV_ROWS, LANES), lambda b, qi: (0, 0)),
                  pl.BlockSpec((4, T, LANES), lambda b, qi: (0, 0, 0)),
                  cache((P, LANES)), cache((P, LANES)), cache((P, 2 * LANES)), cache((P, 2 * LANES)),
                  cache((2, 2, LANES, LANES)), cache((4, LANES)), cache((1, LANES))],
        out_specs=pl.BlockSpec((tq, D_MODEL), lambda b, qi: (b * (T // tq) + qi, 0)),
        out_shape=SDS((N_LAT, D_MODEL), F32),
        scratch_shapes=_mixer_scratch(T),
        compiler_params=pltpu.CompilerParams(vmem_limit_bytes=VMEM_LIMIT,
                                             dimension_semantics=("arbitrary", "arbitrary")),
        name="mixer_lat",
    )(z, pv, rope, ckb, cvb, ckc, cvc, cbd, n0, m0)


def _out_body(x_ref, mix_ref, mod_ref, g_ref, wo_ref, xo_ref, h2_ref):
    y = _dot(mix_ref[...].astype(BF16), wo_ref[...])
    x = x_ref[...] + mod_ref[2:3, :] * y
    xo_ref[...] = x
    h2_ref[...] = _norm_mod(x, g_ref[...], mod_ref[4:5, :], mod_ref[3:4, :]).astype(BF16)


def _out_proj(x, mix, mod, gain, w_out):
    tm = TM_TOK
    return pl.pallas_call(
        _out_body,
        grid=(N_TOK // tm,),
        in_specs=[pl.BlockSpec((tm, D_MODEL), lambda i: (i, 0)),
                  pl.BlockSpec((tm, D_MODEL), lambda i: (i, 0)),
                  pl.BlockSpec((None, 6, D_MODEL), lambda i: (_group_of_tile(i, tm), 0, 0)),
                  pl.BlockSpec((1, D_MODEL), lambda i: (0, 0)),
                  pl.BlockSpec((D_MODEL, D_MODEL), lambda i: (0, 0))],
        out_specs=[pl.BlockSpec((tm, D_MODEL), lambda i: (i, 0)),
                   pl.BlockSpec((tm, D_MODEL), lambda i: (i, 0))],
        out_shape=[SDS((N_TOK, D_MODEL), F32), SDS((N_TOK, D_MODEL), BF16)],
        compiler_params=pltpu.CompilerParams(vmem_limit_bytes=VMEM_LIMIT),
        name="out_proj",
    )(x, mix, mod, gain, w_out)


def _take_top16(s):
    n_rows = s.shape[0]
    ridx = _iota(s.shape, 0)
    rank = jnp.full(s.shape, PEER_TOPK, jnp.int32)
    vals = []
    for a in range(PEER_TOPK):
        mx = jnp.max(s, axis=0, keepdims=True)
        idx = jnp.min(jnp.where(s == mx, ridx, n_rows), axis=0, keepdims=True)
        hit = ridx == idx
        rank = jnp.where(hit, a, rank)
        s = jnp.where(hit, -jnp.inf, s)
        vals.append(mx)
    return vals, rank


_CAND_WIDTH = (16, 8, 5, 4, 3, 2, 2, 2)


def _topk_body(h2_ref, wq_ref, sk_ref, a_ref, cnt_ref, b_ref, r2_ref):
    tl = h2_ref.shape[0]
    qT = _dot_nt(wq_ref[...], h2_ref[...]).astype(BF16)
    s1 = _dot(sk_ref[0], qT[0:N_KEYS])
    s2 = _dot(sk_ref[1], qT[N_KEYS:2 * N_KEYS])
    v1, rank1 = _take_top16(s1)
    v2, rank2 = _take_top16(s2)
    V1 = jnp.concatenate(v1, axis=0)
    V2 = jnp.concatenate(v2, axis=0)
    b8 = _iota((8, tl), 0)
    pieces = [v1[0] + V2, v1[1] + V2[0:8]]
    for a in range(2, 8):
        pieces.append(jnp.where(b8 < _CAND_WIDTH[a], v1[a] + V2[0:8], -jnp.inf))
    pieces.append(V1[8:16] + v2[0])
    cand = jnp.concatenate(pieces, axis=0)
    n_c = cand.shape[0]
    cidx = _iota(cand.shape, 0)
    work = cand
    sel = jnp.zeros(cand.shape, F32)
    for _ in range(PEER_TOPK):
        mx = jnp.max(work, axis=0, keepdims=True)
        idx = jnp.min(jnp.where(work == mx, cidx, n_c), axis=0, keepdims=True)
        hit = cidx == idx
        sel = jnp.where(hit, 1.0, sel)
        work = jnp.where(hit, -jnp.inf, work)
    tmax = v1[0] + v2[0]
    z = jnp.sum(sel * jnp.exp(jnp.where(sel > 0.0, cand, tmax) - tmax), axis=0, keepdims=True)
    cnt = [jnp.sum(sel[0:16], axis=0, keepdims=True), jnp.sum(sel[16:24], axis=0, keepdims=True)]
    for a in range(2, 8):
        cnt.append(jnp.sum(sel[8 * a + 8:8 * a + 16], axis=0, keepdims=True))
    for a in range(8, 16):
        cnt.append(sel[64 + a:65 + a])
    cnt1 = jnp.zeros(s1.shape, F32)
    for a in range(PEER_TOPK):
        cnt1 = jnp.where(rank1 == a, cnt[a], cnt1)
    a_ref[...] = jnp.exp(s1 - v1[0])
    cnt_ref[...] = cnt1
    b_ref[...] = jnp.exp(s2 - v2[0]) * (1.0 / z)
    r2_ref[...] = rank2.astype(F32)


def _peer_topk(h2, wq_t, subkeys):
    tl = TL_TOPK
    outs = SDS((PEER_HEADS, N_KEYS, N_TOK), F32)
    ospec = pl.BlockSpec((None, N_KEYS, tl), lambda i, h: (h, 0, i))
    return pl.pallas_call(
        _topk_body,
        grid=(N_TOK // tl, PEER_HEADS),
        in_specs=[pl.BlockSpec((tl, D_MODEL), lambda i, h: (i, 0)),
                  pl.BlockSpec((2 * N_KEYS, D_MODEL), lambda i, h: (h, 0)),
                  pl.BlockSpec((None, 2, N_KEYS, N_KEYS), lambda i, h: (h, 0, 0, 0))],
        out_specs=[ospec] * 4,
        out_shape=[outs] * 4,
        compiler_params=pltpu.CompilerParams(vmem_limit_bytes=VMEM_LIMIT),
        name="peer_topk",
    )(h2, wq_t, subkeys)


def _peer_body(x_ref, h2_ref, mod_ref, a_ref, cnt_ref, b_ref, r2_ref, u_ref, vt_ref, o_ref, acc_sc, hw_sc):
    eb = pl.program_id(1)
    nt = h2_ref.shape[0]

    @pl.when(eb == 0)
    def _():
        acc_sc[...] = jnp.zeros_like(acc_sc)

    act = _dot_nt(u_ref[...], h2_ref[...])
    gel = 0.5 * act * (1.0 + lax.erf(act * math.sqrt(0.5)))
    rows_per_step = EB_PEER // N_KEYS
    for ii in range(rows_per_step):
        i = eb * rows_per_step + ii
        gate = jnp.zeros((N_KEYS, nt), F32)
        for h in range(PEER_HEADS):
            cnt = cnt_ref[h, pl.ds(i, 1), :]
            a = a_ref[h, pl.ds(i, 1), :]
            gate = gate + jnp.where(r2_ref[h] < cnt, b_ref[h] * a, 0.0)
        hw_sc[N_KEYS * ii:N_KEYS * (ii + 1), :] = (gel[N_KEYS * ii:N_KEYS * (ii + 1), :] * gate).astype(BF16)
    acc_sc[...] += _dot(vt_ref[...], hw_sc[...])

    @pl.when(eb == pl.num_programs(1) - 1)
    def _():
        o_ref[...] = x_ref[...] + mod_ref[5:6, :] * acc_sc[...].T


def _peer_dense(x, h2, mod, at, cntt, bt, r2t, u, vt):
    nt = NT_PEER
    eb = EB_PEER
    fac = pl.BlockSpec((PEER_HEADS, N_KEYS, nt), lambda i, e: (0, 0, i))
    return pl.pallas_call(
        _peer_body,
        grid=(N_TOK // nt, N_EXPERTS // eb),
        in_specs=[pl.BlockSpec((nt, D_MODEL), lambda i, e: (i, 0)),
                  pl.BlockSpec((nt, D_MODEL), lambda i, e: (i, 0)),
                  pl.BlockSpec((None, 6, D_MODEL), lambda i, e: (_group_of_tile(i, nt), 0, 0)),
                  fac, fac, fac, fac,
                  pl.BlockSpec((eb, D_MODEL), lambda i, e: (e, 0)),
                  pl.BlockSpec((D_MODEL, eb), lambda i, e: (0, e))],
        out_specs=pl.BlockSpec((nt, D_MODEL), lambda i, e: (i, 0)),
        out_shape=SDS((N_TOK, D_MODEL), F32),
        scratch_shapes=[pltpu.VMEM((D_MODEL, nt), F32), pltpu.VMEM((eb, nt), BF16)],
        compiler_params=pltpu.CompilerParams(vmem_limit_bytes=VMEM_LIMIT,
                                             dimension_semantics=("arbitrary", "arbitrary")),
        name="peer_dense",
    )(x, h2, mod, at, cntt, bt, r2t, u, vt)


def _rope_tables():
    t = jnp.arange(DEC_SEQ)
    row = (t // GRID_W).astype(F32)[:, None]
    col = (t % GRID_W).astype(F32)[:, None]
    lane = jnp.arange(LANES)

    def tables(dim):
        quarter = dim // 4
        inv = ROPE_BASE ** (-jnp.arange(quarter, dtype=F32) / quarter)
        d = lane % dim
        use_col = (d // (dim // 2)) == 1
        e = d % (dim // 2)
        ang = jnp.where(use_col[None, :], col * inv[e % quarter][None, :], row * inv[e % quarter][None, :])
        sign = jnp.where(e < quarter, -1.0, 1.0).astype(F32)[None, :]
        return jnp.cos(ang), jnp.sin(ang) * sign

    cb, sb = tables(B_HD)
    cc, sc = tables(C_DH)
    return jnp.stack([cb, sb, cc, sc], axis=0)


def _pack_vectors(l, a_out_gain, b_q_gain, b_k_gain, b_sink, c_q_gain, c_k_gain, c_lambda, c_out_gain):
    pad = lambda v: jnp.pad(v, (0, LANES - v.shape[0]))
    rows = [jnp.tile(b_q_gain[l], 2), jnp.tile(b_k_gain[l], 2),
            jnp.tile(c_q_gain[l].reshape(-1), 2), jnp.tile(c_k_gain[l].reshape(-1), 2),
            a_out_gain[l, :LANES], a_out_gain[l, LANES:],
            jnp.tile(c_out_gain[l], 2), pad(b_sink[l])]
    rows += [pad(c_lambda[l, r]) for r in range(4)]
    rows += [jnp.zeros((LANES,), F32)] * (PV_ROWS - len(rows))
    return jnp.stack(rows, axis=0).astype(F32)


def kernel(x_prompt, x_sample, cache_swa_k, cache_swa_v, cache_diff_k, cache_diff_v, state_mlstm_C, state_mlstm_n,
           state_mlstm_m, c, c_ctx, w_ada, b_ada, norm_gain, w_in, b_in, a_out_gain, b_q_gain, b_k_gain, b_sink,
           c_q_gain, c_k_gain, c_lambda, c_out_gain, w_out, peer_wq, peer_subkeys, peer_u, peer_v):
    P = PAST_LEN
    x = jnp.concatenate([x_prompt.reshape(N_CTX, D_MODEL), x_sample.reshape(N_LAT, D_MODEL)], axis=0)
    cond8 = jnp.concatenate([c_ctx[None, :], c, jnp.zeros((5, D_MODEL), F32)], axis=0)
    mod_all = _ada_all(cond8, w_ada, b_ada)[:, :3].reshape(DEPTH, 3, 6, D_MODEL)

    gate_pad = LANES - N_GATES
    w_in_r = jnp.concatenate([w_in[:, :, :GATE_SRC], w_in[:, :, GATE_SRC + N_GATES:],
                              w_in[:, :, GATE_SRC:GATE_SRC + N_GATES],
                              jnp.zeros((DEPTH, D_MODEL, gate_pad), F32)], axis=-1).astype(BF16)
    b_in_r = jnp.concatenate([b_in[:, :GATE_SRC], b_in[:, GATE_SRC + N_GATES:],
                              b_in[:, GATE_SRC:GATE_SRC + N_GATES],
                              jnp.zeros((DEPTH, gate_pad), F32)], axis=-1).reshape(DEPTH, 1, Z_COLS)
    w_out_b = w_out.astype(BF16)
    wq_t = jnp.swapaxes(peer_wq, 1, 2).astype(BF16)
    subkeys_b = peer_subkeys.astype(BF16)
    u_b = peer_u.astype(BF16)
    vt_b = jnp.swapaxes(peer_v.astype(BF16), 1, 2)

    rope = _rope_tables()
    ckb = cache_swa_k.reshape(DEC_BATCH, DEPTH, P, LANES)
    cvb = cache_swa_v.reshape(DEC_BATCH, DEPTH, P, LANES)
    ckc = cache_diff_k.reshape(DEC_BATCH, DEPTH, P, 2 * LANES)
    cvc = cache_diff_v.reshape(DEC_BATCH, DEPTH, P, 2 * LANES)
    sc6 = state_mlstm_C.reshape(DEC_BATCH, DEPTH, 2, 2, 2, A_DK, A_DK)
    zero = jnp.zeros_like(sc6[..., 0, :, :])
    cbd = jnp.concatenate([jnp.concatenate([sc6[..., 0, :, :], zero], axis=-1),
                           jnp.concatenate([zero, sc6[..., 1, :, :]], axis=-1)], axis=-2)
    n0 = state_mlstm_n.reshape(DEC_BATCH, DEPTH, 4, LANES)
    sm = state_mlstm_m
    m0 = jnp.zeros((DEC_BATCH, DEPTH, 1, LANES), F32)
    m0 = m0.at[:, :, 0, 4:8].set(sm[:, :, 0]).at[:, :, 0, 12:16].set(sm[:, :, 1])

    ctx = [[] for _ in range(7)]
    for l in range(DEPTH):
        lam_init = 0.8 - 0.6 * math.exp(-0.3 * l)
        pv = _pack_vectors(l, a_out_gain, b_q_gain, b_k_gain, b_sink, c_q_gain, c_k_gain, c_lambda, c_out_gain)
        z = _in_proj(x, mod_all[l], norm_gain[l, 0:1], w_in_r[l], b_in_r[l])
        mix_c, kb, vb, kc, vc, cst, nst, mst = _mixer_ctx(z, pv, lam_init)
        mix_l = _mixer_lat(z, pv, rope, ckb, cvb, ckc, cvc, cbd, n0, m0, l, lam_init)
        mix = jnp.concatenate([mix_c, mix_l], axis=0)
        x, h2 = _out_proj(x, mix, mod_all[l], norm_gain[l, 1:2], w_out_b[l])
        at, cntt, bt, r2t = _peer_topk(h2, wq_t[l], subkeys_b[l])
        x = _peer_dense(x, h2, mod_all[l], at, cntt, bt, r2t, u_b[l], vt_b[l])
        c4 = jnp.stack([cst[:, :, :, :A_DK, :A_DK], cst[:, :, :, A_DK:, A_DK:]], axis=3)
        for j, v in enumerate((kb, vb, kc, vc, c4.reshape(BATCH, 2, 4, A_DK, A_DK),
                               nst.reshape(BATCH, 2, 4, A_DK),
                               jnp.stack([mst[:, 0, 4:8], mst[:, 0, 12:16]], axis=1))):
            ctx[j].append(v)

    stack = lambda j: jnp.stack(ctx[j], axis=1)
    yp = x[:N_CTX].reshape(BATCH, SEQ, D_MODEL)
    ys = x[N_CTX:].reshape(DEC_BATCH, DEC_SEQ, D_MODEL)
    return (yp, ys,
            stack(0).reshape(BATCH, DEPTH, SEQ, 2, B_HD), stack(1).reshape(BATCH, DEPTH, SEQ, 2, B_HD),
            stack(2).reshape(BATCH, DEPTH, SEQ, 4, 2 * C_DH), stack(3).reshape(BATCH, DEPTH, SEQ, 4, 64),
            stack(4), stack(5), stack(6))
```

```python
import functools
import math

import jax
import jax.numpy as jnp
from jax import lax
from jax.experimental import pallas as pl
from jax.experimental.pallas import tpu as pltpu

F32 = jnp.float32
BF16 = jnp.bfloat16
HIGHEST = lax.Precision.HIGHEST
SDS = jax.ShapeDtypeStruct

D_MODEL = 1024
BATCH = 16
SEQ = 256
DEPTH = 4
DEC_BATCH = 2
DEC_SEQ = 1024
PAST_LEN = 256
GRID_W = 64
WINDOW = 128
ROPE_BASE = 10000.0
EPS = 1e-6
N_KEYS = 128
N_EXPERTS = N_KEYS * N_KEYS
PEER_HEADS = 8
PEER_TOPK = 16
A_DK = 64
B_HD = 64
C_DH = 32

LANES = 128
N_CTX = BATCH * SEQ
N_LAT = DEC_BATCH * DEC_SEQ
N_TOK = N_CTX + N_LAT

COL_AQ, COL_AK, COL_AV, COL_AO = 0, 256, 512, 768
COL_BQ, COL_BK, COL_BV = 1024, 1536, 1664
COL_CQ, COL_CK, COL_CV = 1792, 2048, 2304
COL_AG = 2560
Z_COLS = 2688
GATE_SRC = 1024
N_GATES = 16

PV_BQ, PV_BK, PV_CQ, PV_CK, PV_AO, PV_CO, PV_SINK, PV_LAM = 0, 1, 2, 3, 4, 6, 7, 8
PV_ROWS = 16

TM_TOK = 512
TQ_LAT = 256
TL_TOPK = 256
NT_PEER = 512
EB_PEER = 512
VMEM_LIMIT = 52 * 1024 * 1024

def _iota(shape, dim):
    return lax.broadcasted_iota(jnp.int32, shape, dim)


def _dot_nt(a, b):
    return lax.dot_general(a, b, (((1,), (1,)), ((), ())), preferred_element_type=F32)


def _dot(a, b):
    return jnp.dot(a, b, preferred_element_type=F32)


def _group_of_tile(i, tm):
    n_ctx_tiles = N_CTX // tm
    per_batch = DEC_SEQ // tm
    return jnp.where(i < n_ctx_tiles, 0, 1 + (i - n_ctx_tiles) // per_batch)


def _norm_mod(x, gain, scale, shift):
    y = x * lax.rsqrt(jnp.mean(x * x, axis=-1, keepdims=True) + EPS)
    return (y * gain) * (1.0 + scale) + shift


def _ada_body(c_ref, w_ref, b_ref, o_ref):
    c = c_ref[...]
    a = c * jax.nn.sigmoid(c)
    o_ref[...] = jnp.dot(a, w_ref[...], precision=HIGHEST, preferred_element_type=F32) + b_ref[...]


def _ada_all(cond8, w_ada, b_ada):
    tn = 1536
    return pl.pallas_call(
        _ada_body,
        grid=(DEPTH, 6 * D_MODEL // tn),
        in_specs=[pl.BlockSpec((8, D_MODEL), lambda l, j: (0, 0)),
                  pl.BlockSpec((None, D_MODEL, tn), lambda l, j: (l, 0, j)),
                  pl.BlockSpec((None, 1, tn), lambda l, j: (l, 0, j))],
        out_specs=pl.BlockSpec((None, 8, tn), lambda l, j: (l, 0, j)),
        out_shape=SDS((DEPTH, 8, 6 * D_MODEL), F32),
        compiler_params=pltpu.CompilerParams(vmem_limit_bytes=VMEM_LIMIT),
        name="ada_mod",
    )(cond8, w_ada, b_ada.reshape(DEPTH, 1, 6 * D_MODEL))


def _in_body(x_ref, mod_ref, g_ref, w_ref, b_ref, z_ref):
    h = _norm_mod(x_ref[...], g_ref[...], mod_ref[1:2, :], mod_ref[0:1, :])
    z_ref[...] = _dot(h.astype(BF16), w_ref[...]) + b_ref[...]


def _in_proj(x, mod, gain, w, b, layer):
    tm = TM_TOK
    return pl.pallas_call(
        _in_body,
        grid=(N_TOK // tm,),
        in_specs=[pl.BlockSpec((tm, D_MODEL), lambda i: (i, 0)),
                  pl.BlockSpec((None, 6, D_MODEL), lambda i: (_group_of_tile(i, tm), 0, 0)),
                  pl.BlockSpec((1, D_MODEL), lambda i: (0, 0)),
                  pl.BlockSpec((None, D_MODEL, Z_COLS), lambda i: (layer, 0, 0)),
                  pl.BlockSpec((1, Z_COLS), lambda i: (0, 0))],
        out_specs=pl.BlockSpec((tm, Z_COLS), lambda i: (i, 0)),
        out_shape=SDS((N_TOK, Z_COLS), F32),
        compiler_params=pltpu.CompilerParams(vmem_limit_bytes=VMEM_LIMIT),
        name="in_proj",
    )(x, mod, gain, w, b)


def _block_mean_sq(x, blk):
    r = _iota((LANES, LANES), 0) // blk
    c = _iota((LANES, LANES), 1) // blk
    ones = jnp.where(r == c, 1.0 / blk, 0.0).astype(F32)
    return jnp.dot(x * x, ones, precision=HIGHEST, preferred_element_type=F32)


def _rms(x, gain, blk):
    return x * lax.rsqrt(_block_mean_sq(x, blk) + EPS) * gain


def _rope(x, cos, sin_signed, quarter):
    lane = _iota(x.shape, 1)
    first = (lane % (2 * quarter)) < quarter
    partner = jnp.where(first, pltpu.roll(x, LANES - quarter, axis=1), pltpu.roll(x, quarter, axis=1))
    return x * cos + partner * sin_signed


def _dup_half(x, g):
    lane = _iota(x.shape, 1)
    r = pltpu.roll(x, LANES // 2, axis=1)
    return jnp.where(lane < LANES // 2, x, r) if g == 0 else jnp.where(lane < LANES // 2, r, x)


def _lane_block_mask(shape, start, width):
    lane = _iota(shape, 1)
    return (lane >= start) & (lane < start + width)


def _mixer_body(*refs, latent, tq, T, lam_init):
    if latent:
        (z_ref, pv_ref, rope_ref, ckb_ref, cvb_ref, ckc_ref, cvc_ref, cbd_ref, n0_ref, m0_ref,
         mix_ref, bsel_sc, bselT_sc, zgT_sc, kb_sc, kc_sc) = refs
    else:
        (z_ref, pv_ref, mix_ref, kb_ref, vb_ref, kc_ref, vc_ref, cst_ref, nst_ref, mst_ref,
         bsel_sc, bselT_sc, zgT_sc, kb_sc, kc_sc) = refs

    qi = pl.program_id(1)
    r0 = pl.multiple_of(qi * tq, tq)
    rows = pl.ds(r0, tq)
    half = LANES // 2

    @pl.when(qi == 0)
    def _():
        zg = z_ref[:, COL_AG:COL_AG + LANES]
        ls = jax.nn.log_sigmoid(zg)
        rr = _iota((T, T), 0)
        cc = _iota((T, T), 1)
        tril = (rr >= cc).astype(F32)
        triu = (rr <= cc).astype(F32)
        bf = jnp.dot(tril, ls, precision=HIGHEST, preferred_element_type=F32)
        bb = jnp.dot(triu, ls, precision=HIGHEST, preferred_element_type=F32)
        bsel = jnp.where(_iota((T, LANES), 1) < 8, bf, bb)
        bsel_sc[...] = bsel
        bselT_sc[...] = bsel.T
        zgT_sc[...] = zg.T
        kb = _rms(z_ref[:, COL_BK:COL_BK + LANES], pv_ref[PV_BK:PV_BK + 1, :], B_HD)
        if latent:
            kb = _rope(kb, rope_ref[0], rope_ref[1], B_HD // 4)
        kb_sc[...] = kb
        for p in range(2):
            kc = _rms(z_ref[:, COL_CK + LANES * p:COL_CK + LANES * (p + 1)], pv_ref[PV_CK:PV_CK + 1, :], C_DH)
            if latent:
                kc = _rope(kc, rope_ref[2], rope_ref[3], C_DH // 4)
            kc_sc[p] = kc

    lane_q = _iota((tq, LANES), 1)
    lo_q = lane_q < half
    jq = r0 + _iota((tq, T), 0)
    sk = _iota((tq, T), 1)

    bq = bsel_sc[rows, :]
    if latent:
        bq_m0 = bq + m0_ref[0:1, :]
    else:
        bq_m0 = bq
    for p in range(2):
        q_pair = z_ref[rows, COL_AQ + LANES * p:COL_AQ + LANES * (p + 1)]
        k_pair = (z_ref[:, COL_AK + LANES * p:COL_AK + LANES * (p + 1)] * (A_DK ** -0.5)).astype(BF16)
        v_pair = z_ref[:, COL_AV + LANES * p:COL_AV + LANES * (p + 1)].astype(BF16)
        o_pair = z_ref[rows, COL_AO + LANES * p:COL_AO + LANES * (p + 1)]
        outs = []
        for hh in range(2):
            h = 2 * p + hh
            hmask = lo_q if hh == 0 else jnp.logical_not(lo_q)
            q_h = jnp.where(hmask, q_pair, 0.0)
            qk = _dot_nt(q_h.astype(BF16), k_pair)
            prob = jnp.zeros((tq, T), F32)
            extra = jnp.zeros((tq, LANES), F32)
            for d in range(2):
                c = 4 + 8 * d + h
                bcol = bq[:, c:c + 1]
                brow = bselT_sc[c:c + 1, :]
                irow = zgT_sc[c - 4:c - 3, :]
                causal = (sk <= jq) if d == 0 else (sk >= jq)
                dm = jnp.where(causal, bcol - brow + irow, -jnp.inf)
                bm = bq_m0[:, c:c + 1]
                mrow = jnp.maximum(bm, jnp.max(dm, axis=1, keepdims=True))
                s = qk * jnp.exp(dm - mrow)
                den = jnp.sum(s, axis=1, keepdims=True)
                if latent:
                    inter = jnp.exp(bm - mrow)
                    qn0 = jnp.sum(q_h * n0_ref[2 * d + p:2 * d + p + 1, :], axis=1, keepdims=True)
                    den = den + inter * qn0
                rdn = 1.0 / jnp.maximum(jnp.abs(den), jnp.exp(-mrow))
                prob = prob + s * rdn
                if latent:
                    qc0 = _dot(q_h.astype(BF16), cbd_ref[d, p].astype(BF16))
                    extra = extra + (inter * rdn) * qc0
            outs.append(_dot(prob.astype(BF16), v_pair) + extra)
        hcat = jnp.where(lo_q, outs[0], outs[1])
        hn = _rms(hcat, pv_ref[PV_AO + p:PV_AO + p + 1, :], half)
        mix_ref[:, LANES * p:LANES * (p + 1)] = hn * jax.nn.sigmoid(o_pair)

    if not latent:
        zg = z_ref[:, COL_AG:COL_AG + LANES]
        bsel = bsel_sc[...]
        lane_t = _iota((T, LANES), 1)
        tot = jnp.where(lane_t[0:1, :] < 8, bsel[T - 1:T, :], bsel[0:1, :])
        g = tot - bsel + pltpu.roll(zg, 4, axis=1)
        mfin = jnp.maximum(tot, jnp.max(g, axis=0, keepdims=True))
        w = jnp.exp(g - mfin)
        mst_ref[...] = mfin
        lo_t = lane_t < half
        for d in range(2):
            for p in range(2):
                c0 = 4 + 8 * d + 2 * p
                wsel = jnp.where(lo_t, w[:, c0:c0 + 1], w[:, c0 + 1:c0 + 2])
                k_pair = z_ref[:, COL_AK + LANES * p:COL_AK + LANES * (p + 1)] * (A_DK ** -0.5)
                v_pair = z_ref[:, COL_AV + LANES * p:COL_AV + LANES * (p + 1)]
                kw = k_pair * wsel
                cst_ref[d, p] = lax.dot_general(kw.astype(BF16), v_pair.astype(BF16),
                                                (((0,), (0,)), ((), ())), preferred_element_type=F32)
                nst_ref[2 * d + p:2 * d + p + 1, :] = jnp.sum(kw, axis=0, keepdims=True)

    kb = kb_sc[...]
    vb = z_ref[:, COL_BV:COL_BV + LANES]
    if not latent:
        kb_ref[...] = kb
        vb_ref[...] = vb
    scale_b = B_HD ** -0.5
    for g in range(2):
        kd = _dup_half(kb, g).astype(BF16)
        vd = _dup_half(vb, g).astype(BF16)
        if latent:
            ckd = _dup_half(ckb_ref[...], g).astype(BF16)
            cvd = _dup_half(cvb_ref[...], g).astype(BF16)
            in_window = jnp.abs(jq - sk) <= WINDOW
        for pp in range(2):
            p = 2 * g + pp
            qn = _rms(z_ref[rows, COL_BQ + LANES * p:COL_BQ + LANES * (p + 1)], pv_ref[PV_BQ:PV_BQ + 1, :], B_HD)
            if latent:
                qn = _rope(qn, rope_ref[0, rows, :], rope_ref[1, rows, :], B_HD // 4)
            outs = []
            for hh in range(2):
                hmask = lo_q if hh == 0 else jnp.logical_not(lo_q)
                q_h = jnp.where(hmask, qn, 0.0).astype(BF16)
                sink = pv_ref[PV_SINK:PV_SINK + 1, 2 * p + hh:2 * p + hh + 1]
                s = _dot_nt(q_h, kd) * scale_b
                if latent:
                    s = jnp.where(in_window, s, -jnp.inf)
                    sc = _dot_nt(q_h, ckd) * scale_b
                    m = jnp.maximum(jnp.max(s, axis=1, keepdims=True), jnp.max(sc, axis=1, keepdims=True))
                else:
                    m = jnp.max(s, axis=1, keepdims=True)
                m = jnp.maximum(m, sink)
                e = jnp.exp(s - m)
                l = jnp.sum(e, axis=1, keepdims=True) + jnp.exp(sink - m)
                o = _dot(e.astype(BF16), vd)
                if latent:
                    ec = jnp.exp(sc - m)
                    l = l + jnp.sum(ec, axis=1, keepdims=True)
                    o = o + _dot(ec.astype(BF16), cvd)
                outs.append(o * (1.0 / l))
            mix_ref[:, 256 + LANES * p:256 + LANES * (p + 1)] = jnp.where(lo_q, outs[0], outs[1])

    lp = pv_ref[PV_LAM:PV_LAM + 4, :]
    lam = (jnp.exp(jnp.sum(lp[0:1] * lp[1:2], axis=1, keepdims=True))
           - jnp.exp(jnp.sum(lp[2:3] * lp[3:4], axis=1, keepdims=True)) + lam_init)
    scale_c = C_DH ** -0.5
    for p in range(2):
        kc = kc_sc[p]
        vc = z_ref[:, COL_CV + LANES * p:COL_CV + LANES * (p + 1)]
        if not latent:
            kc_ref[:, LANES * p:LANES * (p + 1)] = kc
            vc_ref[:, LANES * p:LANES * (p + 1)] = vc
        kcb = kc.astype(BF16)
        vcb = vc.astype(BF16)
        if latent:
            ckc = ckc_ref[:, LANES * p:LANES * (p + 1)].astype(BF16)
            cvc = cvc_ref[:, LANES * p:LANES * (p + 1)].astype(BF16)
        qn = _rms(z_ref[rows, COL_CQ + LANES * p:COL_CQ + LANES * (p + 1)], pv_ref[PV_CQ:PV_CQ + 1, :], C_DH)
        if latent:
            qn = _rope(qn, rope_ref[2, rows, :], rope_ref[3, rows, :], C_DH // 4)
        outs = []
        for hh in range(2):
            a_loc = None
            a_ctx = None
            for c in range(2):
                q_m = jnp.where(_lane_block_mask((tq, LANES), half * hh + C_DH * c, C_DH), qn, 0.0).astype(BF16)
                s = _dot_nt(q_m, kcb) * scale_c
                m = jnp.max(s, axis=1, keepdims=True)
                if latent:
                    sc = _dot_nt(q_m, ckc) * scale_c
                    m = jnp.maximum(m, jnp.max(sc, axis=1, keepdims=True))
                e = jnp.exp(s - m)
                l = jnp.sum(e, axis=1, keepdims=True)
                if latent:
                    ec = jnp.exp(sc - m)
                    l = l + jnp.sum(ec, axis=1, keepdims=True)
                rl = 1.0 / l
                if c == 0:
                    a_loc = e * rl
                    if latent:
                        a_ctx = ec * rl
                else:
                    a_loc = a_loc - lam * (e * rl)
                    if latent:
                        a_ctx = a_ctx - lam * (ec * rl)
            o = _dot(a_loc.astype(BF16), vcb)
            if latent:
                o = o + _dot(a_ctx.astype(BF16), cvc)
            outs.append(o)
        ocat = jnp.where(lo_q, outs[0], outs[1])
        mix_ref[:, 768 + LANES * p:768 + LANES * (p + 1)] = (
            _rms(ocat, pv_ref[PV_CO:PV_CO + 1, :], half) * (1.0 - lam_init))


def _mixer_scratch(T):
    return [pltpu.VMEM((T, LANES), F32), pltpu.VMEM((LANES, T), F32), pltpu.VMEM((LANES, T), F32),
            pltpu.VMEM((T, LANES), F32), pltpu.VMEM((2, T, LANES), F32)]


def _mixer_ctx(z, pv, lam_init):
    T = SEQ
    body = functools.partial(_mixer_body, latent=False, tq=T, T=T, lam_init=lam_init)
    per_b = lambda shape: pl.BlockSpec((None,) + shape, lambda b, qi: (b,) + (0,) * len(shape))
    return pl.pallas_call(
        body,
        grid=(BATCH, 1),
        in_specs=[pl.BlockSpec((T, Z_COLS), lambda b, qi: (b, 0)),
                  pl.BlockSpec((PV_ROWS, LANES), lambda b, qi: (0, 0))],
        out_specs=[pl.BlockSpec((T, D_MODEL), lambda b, qi: (b, 0)),
                   per_b((T, LANES)), per_b((T, LANES)), per_b((T, 2 * LANES)), per_b((T, 2 * LANES)),
                   per_b((2, 2, LANES, LANES)), per_b((4, LANES)), per_b((1, LANES))],
        out_shape=[SDS((N_CTX, D_MODEL), F32),
                   SDS((BATCH, T, LANES), F32), SDS((BATCH, T, LANES), F32),
                   SDS((BATCH, T, 2 * LANES), F32), SDS((BATCH, T, 2 * LANES), F32),
                   SDS((BATCH, 2, 2, LANES, LANES), F32), SDS((BATCH, 4, LANES), F32), SDS((BATCH, 1, LANES), F32)],
        scratch_shapes=_mixer_scratch(T),
        compiler_params=pltpu.CompilerParams(vmem_limit_bytes=VMEM_LIMIT,
                                             dimension_semantics=("arbitrary", "arbitrary")),
        name="mixer_ctx",
    )(z, pv)


def _mixer_lat(z, pv, rope, ckb, cvb, ckc, cvc, cbd, n0, m0, layer, lam_init):
    T = DEC_SEQ
    tq = TQ_LAT
    P = PAST_LEN
    body = functools.partial(_mixer_body, latent=True, tq=tq, T=T, lam_init=lam_init)
    ctx_off = N_CTX // T
    cache = lambda shape: pl.BlockSpec((None, None) + shape, lambda b, qi: (b, layer) + (0,) * len(shape))
    return pl.pallas_call(
        body,
        grid=(DEC_BATCH, T // tq),
        in_specs=[pl.BlockSpec((T, Z_COLS), lambda b, qi: (ctx_off + b, 0)),
                  pl.BlockSpec((PV_ROWS, LANES), lambda b, qi: (0, 0)),
                  pl.BlockSpec((4, T, LANES), lambda b, qi: (0, 0, 0)),
                  cache((P, LANES)), cache((P, LANES)), cache((P, 2 * LANES)), cache((P, 2 * LANES)),
                  cache((2, 2, LANES, LANES)), cache((4, LANES)), cache((1, LANES))],
        out_specs=pl.BlockSpec((tq, D_MODEL), lambda b, qi: (b * (T // tq) + qi, 0)),
        out_shape=SDS((N_LAT, D_MODEL), F32),
        scratch_shapes=_mixer_scratch(T),
        compiler_params=pltpu.CompilerParams(vmem_limit_bytes=VMEM_LIMIT,
                                             dimension_semantics=("arbitrary", "arbitrary")),
        name="mixer_lat",
    )(z, pv, rope, ckb, cvb, ckc, cvc, cbd, n0, m0)


def _out_body(x_ref, mixc_ref, mixl_ref, mod_ref, g_ref, wo_ref, xo_ref, h2_ref):
    from_ctx = pl.program_id(0) < N_CTX // TM_TOK
    mix = jnp.where(from_ctx, mixc_ref[...], mixl_ref[...])
    y = _dot(mix.astype(BF16), wo_ref[...])
    x = x_ref[...] + mod_ref[2:3, :] * y
    xo_ref[...] = x
    h2_ref[...] = _norm_mod(x, g_ref[...], mod_ref[4:5, :], mod_ref[3:4, :]).astype(BF16)


def _out_proj(x, mix_c, mix_l, mod, gain, w_out, layer):
    tm = TM_TOK
    n_ctx_tiles = N_CTX // tm
    return pl.pallas_call(
        _out_body,
        grid=(N_TOK // tm,),
        in_specs=[pl.BlockSpec((tm, D_MODEL), lambda i: (i, 0)),
                  pl.BlockSpec((tm, D_MODEL), lambda i: (jnp.minimum(i, n_ctx_tiles - 1), 0)),
                  pl.BlockSpec((tm, D_MODEL), lambda i: (jnp.maximum(i - n_ctx_tiles, 0), 0)),
                  pl.BlockSpec((None, 6, D_MODEL), lambda i: (_group_of_tile(i, tm), 0, 0)),
                  pl.BlockSpec((1, D_MODEL), lambda i: (0, 0)),
                  pl.BlockSpec((None, D_MODEL, D_MODEL), lambda i: (layer, 0, 0))],
        out_specs=[pl.BlockSpec((tm, D_MODEL), lambda i: (i, 0)),
                   pl.BlockSpec((tm, D_MODEL), lambda i: (i, 0))],
        out_shape=[SDS((N_TOK, D_MODEL), F32), SDS((N_TOK, D_MODEL), BF16)],
        compiler_params=pltpu.CompilerParams(vmem_limit_bytes=VMEM_LIMIT),
        name="out_proj",
    )(x, mix_c, mix_l, mod, gain, w_out)


def _take_top16(s):
    n_rows = s.shape[0]
    ridx = _iota(s.shape, 0)
    rank = jnp.full(s.shape, PEER_TOPK, jnp.int32)
    vals = []
    for a in range(PEER_TOPK):
        mx = jnp.max(s, axis=0, keepdims=True)
        idx = jnp.min(jnp.where(s == mx, ridx, n_rows), axis=0, keepdims=True)
        hit = ridx == idx
        rank = jnp.where(hit, a, rank)
        s = jnp.where(hit, -jnp.inf, s)
        vals.append(mx)
    return vals, rank


_CAND_WIDTH = (16, 8, 5, 4, 3, 2, 2, 2)


def _sort16_pairs():
    n, pairs, p = 16, [], 1
    while p < n:
        k = p
        while k >= 1:
            for j in range(k % p, n - k, 2 * k):
                for i in range(min(k, n - j - k)):
                    if (i + j) // (2 * p) == (i + j + k) // (2 * p):
                        pairs.append((i + j, i + j + k))
            k //= 2
        p *= 2
    return pairs


_SORT16 = _sort16_pairs()
SUBLANES = 8


def _sublane_sum(x):
    for d in (4, 2, 1):
        x = x + pltpu.roll(x, d, axis=0)
    return x


def _top16_values(groups):
    g = list(groups)

    def exchange(i, j):
        if g[j] is None:
            return
        if g[i] is None:
            g[i], g[j] = g[j], None
            return
        g[i], g[j] = jnp.maximum(g[i], g[j]), jnp.minimum(g[i], g[j])

    for i, j in _SORT16:
        exchange(i, j)
    for d in (4, 2, 1):
        merged = []
        for v in range(16):
            other = g[15 - v]
            if other is None:
                merged.append(g[v])
            elif g[v] is None:
                merged.append(pltpu.roll(other, d, axis=0))
            else:
                merged.append(jnp.maximum(g[v], pltpu.roll(other, d, axis=0)))
        g = merged
        for stride in (8, 4, 2, 1):
            for i in range(16):
                if (i & stride) == 0:
                    exchange(i, i + stride)
    return g


def _topk_fast(s1, s2, a_ref, cnt_ref, b_ref, r2_ref):
    tl = s1.shape[1]
    g1 = [s1[SUBLANES * v:SUBLANES * (v + 1)] for v in range(16)]
    g2 = [s2[SUBLANES * v:SUBLANES * (v + 1)] for v in range(16)]
    v1 = _top16_values(g1)
    v2 = _top16_values(g2)
    sub = _iota((SUBLANES, tl), 0)
    v2lo, v2hi, v1hi = v2[7], v2[15], v1[15]
    for b in range(6, -1, -1):
        v2lo = jnp.where(sub == b, v2[b], v2lo)
        v2hi = jnp.where(sub == b, v2[8 + b], v2hi)
        v1hi = jnp.where(sub == b, v1[8 + b], v1hi)
    cands = [v1[0] + v2lo, v1[0] + v2hi, v1[1] + v2lo]
    for a in range(2, 8):
        cands.append(jnp.where(sub < _CAND_WIDTH[a], v1[a] + v2lo, -jnp.inf))
    cands.append(v1hi + v2[0])
    tau = _top16_values(cands + [None] * 6)[15]
    tmax = v1[0] + v2[0]
    sel = [(c >= tau).astype(F32) for c in cands]
    cnt = [_sublane_sum(sel[0] + sel[1]), _sublane_sum(sel[2])]
    for a in range(2, 8):
        cnt.append(_sublane_sum(sel[a + 1]))
    for a in range(8, 16):
        cnt.append(((v1[a] + v2[0]) >= tau).astype(F32))
    z = sel[0] * jnp.exp(cands[0] - tmax)
    for k in range(1, len(cands)):
        z = z + sel[k] * jnp.exp(cands[k] - tmax)
    inv_z = 1.0 / _sublane_sum(z)
    total = cnt[0]
    for a in range(1, 16):
        total = total + cnt[a]
    n1 = jnp.zeros((SUBLANES, tl), F32)
    n2 = jnp.zeros((SUBLANES, tl), F32)
    for v in range(16):
        rows = slice(SUBLANES * v, SUBLANES * (v + 1))
        c1 = jnp.zeros((SUBLANES, tl), F32)
        r2 = jnp.full((SUBLANES, tl), float(PEER_TOPK), F32)
        for a in range(PEER_TOPK - 1, -1, -1):
            c1 = jnp.where(g1[v] >= v1[a], cnt[a], c1)
            r2 = jnp.where(g2[v] >= v2[a], float(a), r2)
        n1 = n1 + (g1[v] >= v1[15]).astype(F32)
        n2 = n2 + (g2[v] >= v2[15]).astype(F32)
        a_ref[rows, :] = jnp.exp(g1[v] - v1[0])
        cnt_ref[rows, :] = c1
        b_ref[rows, :] = (jnp.exp(g2[v] - v2[0]) * inv_z).astype(BF16)
        r2_ref[rows, :] = r2.astype(BF16)
    bad = (total != float(PEER_TOPK))
    bad = bad | ((_sublane_sum(n1) != float(PEER_TOPK)) & (cnt[15] > 0.0))
    bad = bad | ((_sublane_sum(n2) != float(PEER_TOPK)) & (cnt[0] >= float(PEER_TOPK)))
    return bad.astype(F32)


def _topk_body(h2_ref, wq_ref, sk_ref, a_ref, cnt_ref, b_ref, r2_ref):
    qT = _dot_nt(wq_ref[...], h2_ref[...]).astype(BF16)
    s1 = _dot(sk_ref[0], qT[0:N_KEYS])
    s2 = _dot(sk_ref[1], qT[N_KEYS:2 * N_KEYS])
    bad = _topk_fast(s1, s2, a_ref, cnt_ref, b_ref, r2_ref)

    @pl.when(jnp.max(bad) > 0.0)
    def _():
        _topk_exact(s1, s2, a_ref, cnt_ref, b_ref, r2_ref)


def _topk_exact(s1, s2, a_ref, cnt_ref, b_ref, r2_ref):
    tl = s1.shape[1]
    v1, rank1 = _take_top16(s1)
    v2, rank2 = _take_top16(s2)
    V1 = jnp.concatenate(v1, axis=0)
    V2 = jnp.concatenate(v2, axis=0)
    b8 = _iota((8, tl), 0)
    pieces = [v1[0] + V2, v1[1] + V2[0:8]]
    for a in range(2, 8):
        pieces.append(jnp.where(b8 < _CAND_WIDTH[a], v1[a] + V2[0:8], -jnp.inf))
    pieces.append(V1[8:16] + v2[0])
    cand = jnp.concatenate(pieces, axis=0)
    n_c = cand.shape[0]
    cidx = _iota(cand.shape, 0)
    work = cand
    sel = jnp.zeros(cand.shape, F32)
    for _ in range(PEER_TOPK):
        mx = jnp.max(work, axis=0, keepdims=True)
        idx = jnp.min(jnp.where(work == mx, cidx, n_c), axis=0, keepdims=True)
        hit = cidx == idx
        sel = jnp.where(hit, 1.0, sel)
        work = jnp.where(hit, -jnp.inf, work)
    tmax = v1[0] + v2[0]
    z = jnp.sum(sel * jnp.exp(jnp.where(sel > 0.0, cand, tmax) - tmax), axis=0, keepdims=True)
    cnt = [jnp.sum(sel[0:16], axis=0, keepdims=True), jnp.sum(sel[16:24], axis=0, keepdims=True)]
    for a in range(2, 8):
        cnt.append(jnp.sum(sel[8 * a + 8:8 * a + 16], axis=0, keepdims=True))
    for a in range(8, 16):
        cnt.append(sel[64 + a:65 + a])
    cnt1 = jnp.zeros(s1.shape, F32)
    for a in range(PEER_TOPK):
        cnt1 = jnp.where(rank1 == a, cnt[a], cnt1)
    a_ref[...] = jnp.exp(s1 - v1[0])
    cnt_ref[...] = cnt1
    b_ref[...] = (jnp.exp(s2 - v2[0]) * (1.0 / z)).astype(BF16)
    r2_ref[...] = rank2.astype(F32).astype(BF16)


def _peer_topk(h2, wq_t, subkeys, layer):
    tl = TL_TOPK
    outs = SDS((PEER_HEADS, N_KEYS, N_TOK), F32)
    outs_b = SDS((PEER_HEADS, N_KEYS, N_TOK), BF16)
    ospec = pl.BlockSpec((None, N_KEYS, tl), lambda i, h: (h, 0, i))
    return pl.pallas_call(
        _topk_body,
        grid=(N_TOK // tl, PEER_HEADS),
        in_specs=[pl.BlockSpec((tl, D_MODEL), lambda i, h: (i, 0)),
                  pl.BlockSpec((None, 2 * N_KEYS, D_MODEL), lambda i, h: (layer, h, 0)),
                  pl.BlockSpec((None, None, 2, N_KEYS, N_KEYS), lambda i, h: (layer, h, 0, 0, 0))],
        out_specs=[ospec] * 4,
        out_shape=[outs, outs, outs_b, outs_b],
        compiler_params=pltpu.CompilerParams(vmem_limit_bytes=VMEM_LIMIT),
        name="peer_topk",
    )(h2, wq_t, subkeys)


def _peer_step(cur, s, h2_ref, a_ref, cnt_ref, b_ref, r2_ref, u_ref, vt_ref, acc_sc, act_sc, hw_sc):
    prev = 1 - cur
    n_blocks = pl.num_programs(1) - 2
    nt = h2_ref.shape[0]
    valid = jnp.logical_and(s >= 1, s <= n_blocks).astype(F32)
    blk = jnp.clip(s - 1, 0, n_blocks - 1)
    rows_per_step = EB_PEER // N_KEYS

    cnt_rows = [[(cnt_ref[h, pl.ds(blk * rows_per_step + ii, 1), :] * valid).astype(BF16)
                 for h in range(PEER_HEADS)] for ii in range(rows_per_step)]
    a_rows = [[a_ref[h, pl.ds(blk * rows_per_step + ii, 1), :].astype(BF16)
               for h in range(PEER_HEADS)] for ii in range(rows_per_step)]
    for lc in range(nt // (2 * LANES)):
        tc = slice(2 * LANES * lc, 2 * LANES * (lc + 1))
        acc_sc[:, tc] += _dot(vt_ref[...], hw_sc[cur, :, tc])
        for ii in range(rows_per_step):
            er = slice(N_KEYS * ii, N_KEYS * (ii + 1))
            gate = jnp.zeros((N_KEYS, 2 * LANES), BF16)
            for h in range(PEER_HEADS):
                gate = gate + jnp.where(r2_ref[h, :, tc] < cnt_rows[ii][h][:, tc],
                                        b_ref[h, :, tc] * a_rows[ii][h][:, tc], jnp.zeros((), BF16))
            act = act_sc[prev, er, tc]
            gel = 0.5 * act * (1.0 + lax.erf(act * math.sqrt(0.5)))
            hw_sc[prev, er, tc] = gel.astype(BF16) * gate
        act_sc[cur, :, tc] = _dot_nt(u_ref[...], h2_ref[tc, :])


def _peer_body(x_ref, h2_ref, mod_ref, a_ref, cnt_ref, b_ref, r2_ref, u_ref, vt_ref, o_ref, acc_sc, act_sc, hw_sc):
    s = pl.program_id(1)

    @pl.when(s == 0)
    def _():
        acc_sc[...] = jnp.zeros_like(acc_sc)
        act_sc[...] = jnp.zeros_like(act_sc)
        hw_sc[...] = jnp.zeros_like(hw_sc)

    _peer_step(s % 2, s, h2_ref, a_ref, cnt_ref, b_ref, r2_ref, u_ref, vt_ref, acc_sc, act_sc, hw_sc)

    @pl.when(s == pl.num_programs(1) - 1)
    def _():
        o_ref[...] = x_ref[...] + mod_ref[5:6, :] * acc_sc[...].T


def _peer_dense(x, h2, mod, at, cntt, bt, r2t, u, vt, layer):
    nt = NT_PEER
    eb = EB_PEER
    n_blocks = N_EXPERTS // eb
    fac = pl.BlockSpec((PEER_HEADS, N_KEYS, nt), lambda i, s: (0, 0, i))
    return pl.pallas_call(
        _peer_body,
        grid=(N_TOK // nt, n_blocks + 2),
        in_specs=[pl.BlockSpec((nt, D_MODEL), lambda i, s: (i, 0)),
                  pl.BlockSpec((nt, D_MODEL), lambda i, s: (i, 0)),
                  pl.BlockSpec((None, 6, D_MODEL), lambda i, s: (_group_of_tile(i, nt), 0, 0)),
                  fac, fac, fac, fac,
                  pl.BlockSpec((None, eb, D_MODEL), lambda i, s: (layer, jnp.minimum(s, n_blocks - 1), 0)),
                  pl.BlockSpec((None, D_MODEL, eb), lambda i, s: (layer, 0, jnp.maximum(s - 2, 0)))],
        out_specs=pl.BlockSpec((nt, D_MODEL), lambda i, s: (i, 0)),
        out_shape=SDS((N_TOK, D_MODEL), F32),
        scratch_shapes=[pltpu.VMEM((D_MODEL, nt), F32), pltpu.VMEM((2, eb, nt), F32),
                        pltpu.VMEM((2, eb, nt), BF16)],
        compiler_params=pltpu.CompilerParams(vmem_limit_bytes=VMEM_LIMIT,
                                             dimension_semantics=("arbitrary", "arbitrary")),
        name="peer_dense",
    )(x, h2, mod, at, cntt, bt, r2t, u, vt)


def _rope_tables():
    t = jnp.arange(DEC_SEQ)
    row = (t // GRID_W).astype(F32)[:, None]
    col = (t % GRID_W).astype(F32)[:, None]
    lane = jnp.arange(LANES)

    def tables(dim):
        quarter = dim // 4
        inv = ROPE_BASE ** (-jnp.arange(quarter, dtype=F32) / quarter)
        d = lane % dim
        use_col = (d // (dim // 2)) == 1
        e = d % (dim // 2)
        ang = jnp.where(use_col[None, :], col * inv[e % quarter][None, :], row * inv[e % quarter][None, :])
        sign = jnp.where(e < quarter, -1.0, 1.0).astype(F32)[None, :]
        return jnp.cos(ang), jnp.sin(ang) * sign

    cb, sb = tables(B_HD)
    cc, sc = tables(C_DH)
    return jnp.stack([cb, sb, cc, sc], axis=0)


def _pack_vectors(l, a_out_gain, b_q_gain, b_k_gain, b_sink, c_q_gain, c_k_gain, c_lambda, c_out_gain):
    pad = lambda v: jnp.pad(v, (0, LANES - v.shape[0]))
    rows = [jnp.tile(b_q_gain[l], 2), jnp.tile(b_k_gain[l], 2),
            jnp.tile(c_q_gain[l].reshape(-1), 2), jnp.tile(c_k_gain[l].reshape(-1), 2),
            a_out_gain[l, :LANES], a_out_gain[l, LANES:],
            jnp.tile(c_out_gain[l], 2), pad(b_sink[l])]
    rows += [pad(c_lambda[l, r]) for r in range(4)]
    rows += [jnp.zeros((LANES,), F32)] * (PV_ROWS - len(rows))
    return jnp.stack(rows, axis=0).astype(F32)


def kernel(x_prompt, x_sample, cache_swa_k, cache_swa_v, cache_diff_k, cache_diff_v, state_mlstm_C, state_mlstm_n,
           state_mlstm_m, c, c_ctx, w_ada, b_ada, norm_gain, w_in, b_in, a_out_gain, b_q_gain, b_k_gain, b_sink,
           c_q_gain, c_k_gain, c_lambda, c_out_gain, w_out, peer_wq, peer_subkeys, peer_u, peer_v):
    P = PAST_LEN
    x = jnp.concatenate([x_prompt.reshape(N_CTX, D_MODEL), x_sample.reshape(N_LAT, D_MODEL)], axis=0)
    cond8 = jnp.concatenate([c_ctx[None, :], c, jnp.zeros((5, D_MODEL), F32)], axis=0)
    mod_all = _ada_all(cond8, w_ada, b_ada)[:, :3].reshape(DEPTH, 3, 6, D_MODEL)

    gate_pad = LANES - N_GATES
    w_in_r = jnp.concatenate([w_in[:, :, :GATE_SRC], w_in[:, :, GATE_SRC + N_GATES:],
                              w_in[:, :, GATE_SRC:GATE_SRC + N_GATES],
                              jnp.zeros((DEPTH, D_MODEL, gate_pad), F32)], axis=-1).astype(BF16)
    b_in_r = jnp.concatenate([b_in[:, :GATE_SRC], b_in[:, GATE_SRC + N_GATES:],
                              b_in[:, GATE_SRC:GATE_SRC + N_GATES],
                              jnp.zeros((DEPTH, gate_pad), F32)], axis=-1).reshape(DEPTH, 1, Z_COLS)
    w_out_b = w_out.astype(BF16)
    wq_t = jnp.swapaxes(peer_wq, 1, 2).astype(BF16)
    subkeys_b = peer_subkeys.astype(BF16)
    u_b = peer_u.astype(BF16)
    vt_b = jnp.swapaxes(peer_v.astype(BF16), 1, 2)

    rope = _rope_tables()
    ckb = cache_swa_k.reshape(DEC_BATCH, DEPTH, P, LANES)
    cvb = cache_swa_v.reshape(DEC_BATCH, DEPTH, P, LANES)
    ckc = cache_diff_k.reshape(DEC_BATCH, DEPTH, P, 2 * LANES)
    cvc = cache_diff_v.reshape(DEC_BATCH, DEPTH, P, 2 * LANES)
    sc6 = state_mlstm_C.reshape(DEC_BATCH, DEPTH, 2, 2, 2, A_DK, A_DK)
    zero = jnp.zeros_like(sc6[..., 0, :, :])
    cbd = jnp.concatenate([jnp.concatenate([sc6[..., 0, :, :], zero], axis=-1),
                           jnp.concatenate([zero, sc6[..., 1, :, :]], axis=-1)], axis=-2)
    n0 = state_mlstm_n.reshape(DEC_BATCH, DEPTH, 4, LANES)
    sm = state_mlstm_m
    m0 = jnp.zeros((DEC_BATCH, DEPTH, 1, LANES), F32)
    m0 = m0.at[:, :, 0, 4:8].set(sm[:, :, 0]).at[:, :, 0, 12:16].set(sm[:, :, 1])

    ctx = [[] for _ in range(7)]
    for l in range(DEPTH):
        lam_init = 0.8 - 0.6 * math.exp(-0.3 * l)
        pv = _pack_vectors(l, a_out_gain, b_q_gain, b_k_gain, b_sink, c_q_gain, c_k_gain, c_lambda, c_out_gain)
        z = _in_proj(x, mod_all[l], norm_gain[l, 0:1], w_in_r, b_in_r[l], l)
        mix_c, kb, vb, kc, vc, cst, nst, mst = _mixer_ctx(z, pv, lam_init)
        mix_l = _mixer_lat(z, pv, rope, ckb, cvb, ckc, cvc, cbd, n0, m0, l, lam_init)
        x, h2 = _out_proj(x, mix_c, mix_l, mod_all[l], norm_gain[l, 1:2], w_out_b, l)
        at, cntt, bt, r2t = _peer_topk(h2, wq_t, subkeys_b, l)
        x = _peer_dense(x, h2, mod_all[l], at, cntt, bt, r2t, u_b, vt_b, l)
        c4 = jnp.stack([cst[:, :, :, :A_DK, :A_DK], cst[:, :, :, A_DK:, A_DK:]], axis=3)
        for j, v in enumerate((kb, vb, kc, vc, c4.reshape(BATCH, 2, 4, A_DK, A_DK),
                               nst.reshape(BATCH, 2, 4, A_DK),
                               jnp.stack([mst[:, 0, 4:8], mst[:, 0, 12:16]], axis=1))):
            ctx[j].append(v)

    stack = lambda j: jnp.stack(ctx[j], axis=1)
    yp = x[:N_CTX].reshape(BATCH, SEQ, D_MODEL)
    ys = x[N_CTX:].reshape(DEC_BATCH, DEC_SEQ, D_MODEL)
    return (yp, ys,
            stack(0).reshape(BATCH, DEPTH, SEQ, 2, B_HD), stack(1).reshape(BATCH, DEPTH, SEQ, 2, B_HD),
            stack(2).reshape(BATCH, DEPTH, SEQ, 4, 2 * C_DH), stack(3).reshape(BATCH, DEPTH, SEQ, 4, 64),
            stack(4), stack(5), stack(6))
```

```python
import functools
import math

import jax
import jax.numpy as jnp
from jax import lax
from jax.experimental import pallas as pl
from jax.experimental.pallas import tpu as pltpu

F32 = jnp.float32
BF16 = jnp.bfloat16
HIGHEST = lax.Precision.HIGHEST
SDS = jax.ShapeDtypeStruct

D_MODEL = 1024
BATCH = 16
SEQ = 256
DEPTH = 4
DEC_BATCH = 2
DEC_SEQ = 1024
PAST_LEN = 256
GRID_W = 64
WINDOW = 128
ROPE_BASE = 10000.0
EPS = 1e-6
N_KEYS = 128
N_EXPERTS = N_KEYS * N_KEYS
PEER_HEADS = 8
PEER_TOPK = 16
A_DK = 64
B_HD = 64
C_DH = 32

LANES = 128
N_CTX = BATCH * SEQ
N_LAT = DEC_BATCH * DEC_SEQ
N_TOK = N_CTX + N_LAT

COL_AQ, COL_AK, COL_AV, COL_AO = 0, 256, 512, 768
COL_BQ, COL_BK, COL_BV = 1024, 1536, 1664
COL_CQ, COL_CK, COL_CV = 1792, 2048, 2304
COL_AG = 2560
Z_COLS = 2688
GATE_SRC = 1024
N_GATES = 16

PV_BQ, PV_BK, PV_CQ, PV_CK, PV_AO, PV_CO, PV_SINK, PV_LAM = 0, 1, 2, 3, 4, 6, 7, 8
PV_ROWS = 16

TM_TOK = 512
TQ_LAT = 256
TL_TOPK = 256
NT_PEER = 512
EB_PEER = 512
VMEM_LIMIT = 52 * 1024 * 1024

def _iota(shape, dim):
    return lax.broadcasted_iota(jnp.int32, shape, dim)


def _dot_nt(a, b):
    return lax.dot_general(a, b, (((1,), (1,)), ((), ())), preferred_element_type=F32)


def _dot(a, b):
    return jnp.dot(a, b, preferred_element_type=F32)


def _group_of_tile(i, tm):
    n_ctx_tiles = N_CTX // tm
    per_batch = DEC_SEQ // tm
    return jnp.where(i < n_ctx_tiles, 0, 1 + (i - n_ctx_tiles) // per_batch)


def _norm_mod(x, gain, scale, shift):
    y = x * lax.rsqrt(jnp.mean(x * x, axis=-1, keepdims=True) + EPS)
    return (y * gain) * (1.0 + scale) + shift


def _ada_body(c_ref, w_ref, b_ref, o_ref):
    c = c_ref[...]
    a = c * jax.nn.sigmoid(c)
    o_ref[...] = jnp.dot(a, w_ref[...], precision=HIGHEST, preferred_element_type=F32) + b_ref[...]


def _ada_all(cond8, w_ada, b_ada):
    tn = 1536
    return pl.pallas_call(
        _ada_body,
        grid=(DEPTH, 6 * D_MODEL // tn),
        in_specs=[pl.BlockSpec((8, D_MODEL), lambda l, j: (0, 0)),
                  pl.BlockSpec((None, D_MODEL, tn), lambda l, j: (l, 0, j)),
                  pl.BlockSpec((None, 1, tn), lambda l, j: (l, 0, j))],
        out_specs=pl.BlockSpec((None, 8, tn), lambda l, j: (l, 0, j)),
        out_shape=SDS((DEPTH, 8, 6 * D_MODEL), F32),
        compiler_params=pltpu.CompilerParams(vmem_limit_bytes=VMEM_LIMIT),
        name="ada_mod",
    )(cond8, w_ada, b_ada.reshape(DEPTH, 1, 6 * D_MODEL))


def _in_body(x_ref, mod_ref, g_ref, w_ref, b_ref, z_ref):
    h = _norm_mod(x_ref[...], g_ref[...], mod_ref[1:2, :], mod_ref[0:1, :])
    z_ref[...] = _dot(h.astype(BF16), w_ref[...]) + b_ref[...]


def _in_proj(x, mod, gain, w, b, layer):
    tm = TM_TOK
    return pl.pallas_call(
        _in_body,
        grid=(N_TOK // tm,),
        in_specs=[pl.BlockSpec((tm, D_MODEL), lambda i: (i, 0)),
                  pl.BlockSpec((None, 6, D_MODEL), lambda i: (_group_of_tile(i, tm), 0, 0)),
                  pl.BlockSpec((1, D_MODEL), lambda i: (0, 0)),
                  pl.BlockSpec((None, D_MODEL, Z_COLS), lambda i: (layer, 0, 0)),
                  pl.BlockSpec((1, Z_COLS), lambda i: (0, 0))],
        out_specs=pl.BlockSpec((tm, Z_COLS), lambda i: (i, 0)),
        out_shape=SDS((N_TOK, Z_COLS), F32),
        compiler_params=pltpu.CompilerParams(vmem_limit_bytes=VMEM_LIMIT),
        name="in_proj",
    )(x, mod, gain, w, b)


def _block_mean_sq(x, blk):
    r = _iota((LANES, LANES), 0) // blk
    c = _iota((LANES, LANES), 1) // blk
    ones = jnp.where(r == c, 1.0 / blk, 0.0).astype(F32)
    return jnp.dot(x * x, ones, precision=HIGHEST, preferred_element_type=F32)


def _rms(x, gain, blk):
    return x * lax.rsqrt(_block_mean_sq(x, blk) + EPS) * gain


def _rope(x, cos, sin_signed, quarter):
    lane = _iota(x.shape, 1)
    first = (lane % (2 * quarter)) < quarter
    partner = jnp.where(first, pltpu.roll(x, LANES - quarter, axis=1), pltpu.roll(x, quarter, axis=1))
    return x * cos + partner * sin_signed


def _dup_half(x, g):
    lane = _iota(x.shape, 1)
    r = pltpu.roll(x, LANES // 2, axis=1)
    return jnp.where(lane < LANES // 2, x, r) if g == 0 else jnp.where(lane < LANES // 2, r, x)


def _lane_block_mask(shape, start, width):
    lane = _iota(shape, 1)
    return (lane >= start) & (lane < start + width)


def _mixer_body(*refs, latent, tq, T, lam_init):
    if latent:
        (z_ref, pv_ref, rope_ref, ckb_ref, cvb_ref, ckc_ref, cvc_ref, cbd_ref, n0_ref, m0_ref,
         mix_ref, bsel_sc, bselT_sc, zgT_sc, kb_sc, kc_sc) = refs
    else:
        (z_ref, pv_ref, mix_ref, kb_ref, vb_ref, kc_ref, vc_ref, cst_ref, nst_ref, mst_ref,
         bsel_sc, bselT_sc, zgT_sc, kb_sc, kc_sc) = refs

    qi = pl.program_id(1)
    r0 = pl.multiple_of(qi * tq, tq)
    rows = pl.ds(r0, tq)
    half = LANES // 2

    @pl.when(qi == 0)
    def _():
        zg = z_ref[:, COL_AG:COL_AG + LANES]
        ls = jax.nn.log_sigmoid(zg)
        rr = _iota((LANES, LANES), 0)
        cc = _iota((LANES, LANES), 1)
        tril = (rr >= cc).astype(F32)
        triu = (rr <= cc).astype(F32)
        n_blk = T // LANES
        pre = [jnp.dot(tril, ls[LANES * k:LANES * (k + 1)], precision=HIGHEST, preferred_element_type=F32)
               for k in range(n_blk)]
        suf = [jnp.dot(triu, ls[LANES * k:LANES * (k + 1)], precision=HIGHEST, preferred_element_type=F32)
               for k in range(n_blk)]
        tot = [pre[k][LANES - 1:LANES, :] for k in range(n_blk)]
        fwd_lane = _iota((LANES, LANES), 1) < 8
        off_f = jnp.zeros((1, LANES), F32)
        off_b = [jnp.zeros((1, LANES), F32)] * n_blk
        for k in range(n_blk - 2, -1, -1):
            off_b[k] = off_b[k + 1] + tot[k + 1]
        for k in range(n_blk):
            bsel_sc[LANES * k:LANES * (k + 1), :] = jnp.where(fwd_lane, pre[k] + off_f, suf[k] + off_b[k])
            off_f = off_f + tot[k]
        bselT_sc[...] = bsel_sc[...].T
        zgT_sc[...] = zg.T
        kb = _rms(z_ref[:, COL_BK:COL_BK + LANES], pv_ref[PV_BK:PV_BK + 1, :], B_HD)
        if latent:
            kb = _rope(kb, rope_ref[0], rope_ref[1], B_HD // 4)
        kb_sc[...] = kb
        for p in range(2):
            kc = _rms(z_ref[:, COL_CK + LANES * p:COL_CK + LANES * (p + 1)], pv_ref[PV_CK:PV_CK + 1, :], C_DH)
            if latent:
                kc = _rope(kc, rope_ref[2], rope_ref[3], C_DH // 4)
            kc_sc[p] = kc

    lane_q = _iota((tq, LANES), 1)
    lo_q = lane_q < half
    hmasks = (lo_q, jnp.logical_not(lo_q))
    jq = r0 + _iota((tq, T), 0)
    sk = _iota((tq, T), 1)
    pair_groups = ((0,), (1,)) if latent else ((0, 1),)

    bq = bsel_sc[rows, :]
    bq_m0 = bq + m0_ref[0:1, :] if latent else bq
    causal = ((sk <= jq), (sk >= jq))
    for ps in pair_groups:
        heads = [(p, hh) for p in ps for hh in range(2)]
        chains = [(p, hh, d) for p, hh in heads for d in range(2)]
        q_pair = {p: z_ref[rows, COL_AQ + LANES * p:COL_AQ + LANES * (p + 1)] for p in ps}
        k_pair = {p: (z_ref[:, COL_AK + LANES * p:COL_AK + LANES * (p + 1)] * (A_DK ** -0.5)).astype(BF16) for p in ps}
        v_pair = {p: z_ref[:, COL_AV + LANES * p:COL_AV + LANES * (p + 1)].astype(BF16) for p in ps}
        q_h = {(p, hh): jnp.where(hmasks[hh], q_pair[p], 0.0) for p, hh in heads}
        qk = {ph: _dot_nt(q_h[ph].astype(BF16), k_pair[ph[0]]) for ph in heads}
        col = {(p, hh, d): 4 + 8 * d + 2 * p + hh for p, hh, d in chains}
        dm = {ch: jnp.where(causal[ch[2]], bq[:, col[ch]:col[ch] + 1] - bselT_sc[col[ch]:col[ch] + 1, :]
                            + zgT_sc[col[ch] - 4:col[ch] - 3, :], -jnp.inf) for ch in chains}
        bm = {ch: bq_m0[:, col[ch]:col[ch] + 1] for ch in chains}
        mrow = {ch: jnp.maximum(bm[ch], jnp.max(dm[ch], axis=1, keepdims=True)) for ch in chains}
        s = {ch: qk[ch[:2]] * jnp.exp(dm[ch] - mrow[ch]) for ch in chains}
        den = {ch: jnp.sum(s[ch], axis=1, keepdims=True) for ch in chains}
        if latent:
            inter = {ch: jnp.exp(bm[ch] - mrow[ch]) for ch in chains}
            qn0 = {ch: jnp.sum(q_h[ch[:2]] * n0_ref[2 * ch[2] + ch[0]:2 * ch[2] + ch[0] + 1, :], axis=1, keepdims=True)
                   for ch in chains}
            den = {ch: den[ch] + inter[ch] * qn0[ch] for ch in chains}
            qc0 = {ch: _dot(q_h[ch[:2]].astype(BF16), cbd_ref[ch[2], ch[0]].astype(BF16)) for ch in chains}
        rdn = {ch: 1.0 / jnp.maximum(jnp.abs(den[ch]), jnp.exp(-mrow[ch])) for ch in chains}
        prob = {ph: s[ph + (0,)] * rdn[ph + (0,)] + s[ph + (1,)] * rdn[ph + (1,)] for ph in heads}
        outs = {ph: _dot(prob[ph].astype(BF16), v_pair[ph[0]]) for ph in heads}
        if latent:
            outs = {ph: outs[ph] + (inter[ph + (0,)] * rdn[ph + (0,)]) * qc0[ph + (0,)]
                    + (inter[ph + (1,)] * rdn[ph + (1,)]) * qc0[ph + (1,)] for ph in heads}
        for p in ps:
            hcat = jnp.where(lo_q, outs[(p, 0)], outs[(p, 1)])
            hn = _rms(hcat, pv_ref[PV_AO + p:PV_AO + p + 1, :], half)
            o_pair = z_ref[rows, COL_AO + LANES * p:COL_AO + LANES * (p + 1)]
            mix_ref[:, LANES * p:LANES * (p + 1)] = hn * jax.nn.sigmoid(o_pair)

    if not latent:
        zg = z_ref[:, COL_AG:COL_AG + LANES]
        bsel = bsel_sc[...]
        lane_t = _iota((T, LANES), 1)
        tot = jnp.where(lane_t[0:1, :] < 8, bsel[T - 1:T, :], bsel[0:1, :])
        g = tot - bsel + pltpu.roll(zg, 4, axis=1)
        mfin = jnp.maximum(tot, jnp.max(g, axis=0, keepdims=True))
        w = jnp.exp(g - mfin)
        mst_ref[...] = mfin
        lo_t = lane_t < half
        for d in range(2):
            for p in range(2):
                c0 = 4 + 8 * d + 2 * p
                wsel = jnp.where(lo_t, w[:, c0:c0 + 1], w[:, c0 + 1:c0 + 2])
                k_pair = z_ref[:, COL_AK + LANES * p:COL_AK + LANES * (p + 1)] * (A_DK ** -0.5)
                v_pair = z_ref[:, COL_AV + LANES * p:COL_AV + LANES * (p + 1)]
                kw = k_pair * wsel
                cst_ref[d, p] = lax.dot_general(kw.astype(BF16), v_pair.astype(BF16),
                                                (((0,), (0,)), ((), ())), preferred_element_type=F32)
                nst_ref[2 * d + p:2 * d + p + 1, :] = jnp.sum(kw, axis=0, keepdims=True)

    kb = kb_sc[...]
    vb = z_ref[:, COL_BV:COL_BV + LANES]
    if not latent:
        kb_ref[...] = kb
        vb_ref[...] = vb
    scale_b = B_HD ** -0.5
    if latent:
        k0 = pl.multiple_of(jnp.clip(r0 - WINDOW, 0, T - 2 * tq), LANES)
        kb = kb_sc[pl.ds(k0, 2 * tq), :]
        vb = z_ref[pl.ds(k0, 2 * tq), COL_BV:COL_BV + LANES]
        jq_w = r0 + _iota((tq, 2 * tq), 0)
        sk_w = k0 + _iota((tq, 2 * tq), 1)
        in_window = jnp.abs(jq_w - sk_w) <= WINDOW
    for gs in (((0,), (1,)) if latent else ((0, 1),)):
        heads = [(g, pp, hh) for g in gs for pp in range(2) for hh in range(2)]
        kd = {g: _dup_half(kb, g).astype(BF16) for g in gs}
        vd = {g: _dup_half(vb, g).astype(BF16) for g in gs}
        if latent:
            ckd = {g: _dup_half(ckb_ref[...], g).astype(BF16) for g in gs}
            cvd = {g: _dup_half(cvb_ref[...], g).astype(BF16) for g in gs}
        qn = {}
        for g in gs:
            for pp in range(2):
                p = 2 * g + pp
                q = _rms(z_ref[rows, COL_BQ + LANES * p:COL_BQ + LANES * (p + 1)], pv_ref[PV_BQ:PV_BQ + 1, :], B_HD)
                if latent:
                    q = _rope(q, rope_ref[0, rows, :], rope_ref[1, rows, :], B_HD // 4)
                qn[(g, pp)] = q
        q_h = {h: jnp.where(hmasks[h[2]], qn[h[:2]], 0.0).astype(BF16) for h in heads}
        sink = {h: pv_ref[PV_SINK:PV_SINK + 1, 4 * h[0] + 2 * h[1] + h[2]:4 * h[0] + 2 * h[1] + h[2] + 1] for h in heads}
        s = {h: _dot_nt(q_h[h], kd[h[0]]) * scale_b for h in heads}
        if latent:
            s = {h: jnp.where(in_window, s[h], -jnp.inf) for h in heads}
            sc = {h: _dot_nt(q_h[h], ckd[h[0]]) * scale_b for h in heads}
            m = {h: jnp.maximum(jnp.max(s[h], axis=1, keepdims=True), jnp.max(sc[h], axis=1, keepdims=True))
                 for h in heads}
        else:
            m = {h: jnp.max(s[h], axis=1, keepdims=True) for h in heads}
        m = {h: jnp.maximum(m[h], sink[h]) for h in heads}
        e = {h: jnp.exp(s[h] - m[h]) for h in heads}
        l = {h: jnp.sum(e[h], axis=1, keepdims=True) + jnp.exp(sink[h] - m[h]) for h in heads}
        o = {h: _dot(e[h].astype(BF16), vd[h[0]]) for h in heads}
        if latent:
            ec = {h: jnp.exp(sc[h] - m[h]) for h in heads}
            l = {h: l[h] + jnp.sum(ec[h], axis=1, keepdims=True) for h in heads}
            o = {h: o[h] + _dot(ec[h].astype(BF16), cvd[h[0]]) for h in heads}
        o = {h: o[h] * (1.0 / l[h]) for h in heads}
        for g in gs:
            for pp in range(2):
                p = 2 * g + pp
                mix_ref[:, 256 + LANES * p:256 + LANES * (p + 1)] = jnp.where(lo_q, o[(g, pp, 0)], o[(g, pp, 1)])

    lp = pv_ref[PV_LAM:PV_LAM + 4, :]
    lam = (jnp.exp(jnp.sum(lp[0:1] * lp[1:2], axis=1, keepdims=True))
           - jnp.exp(jnp.sum(lp[2:3] * lp[3:4], axis=1, keepdims=True)) + lam_init)
    scale_c = C_DH ** -0.5
    for ps in pair_groups:
        maps = [(p, hh, c) for p in ps for hh in range(2) for c in range(2)]
        heads = [(p, hh) for p in ps for hh in range(2)]
        kc = {p: kc_sc[p] for p in ps}
        vc = {p: z_ref[:, COL_CV + LANES * p:COL_CV + LANES * (p + 1)] for p in ps}
        if not latent:
            for p in ps:
                kc_ref[:, LANES * p:LANES * (p + 1)] = kc[p]
                vc_ref[:, LANES * p:LANES * (p + 1)] = vc[p]
        kcb = {p: kc[p].astype(BF16) for p in ps}
        vcb = {p: vc[p].astype(BF16) for p in ps}
        if latent:
            ckc = {p: ckc_ref[:, LANES * p:LANES * (p + 1)].astype(BF16) for p in ps}
            cvc = {p: cvc_ref[:, LANES * p:LANES * (p + 1)].astype(BF16) for p in ps}
        qn = {}
        for p in ps:
            q = _rms(z_ref[rows, COL_CQ + LANES * p:COL_CQ + LANES * (p + 1)], pv_ref[PV_CQ:PV_CQ + 1, :], C_DH)
            if latent:
                q = _rope(q, rope_ref[2, rows, :], rope_ref[3, rows, :], C_DH // 4)
            qn[p] = q
        q_m = {mp: jnp.where(_lane_block_mask((tq, LANES), half * mp[1] + C_DH * mp[2], C_DH), qn[mp[0]], 0.0).astype(BF16)
               for mp in maps}
        s = {mp: _dot_nt(q_m[mp], kcb[mp[0]]) * scale_c for mp in maps}
        m = {mp: jnp.max(s[mp], axis=1, keepdims=True) for mp in maps}
        if latent:
            sc = {mp: _dot_nt(q_m[mp], ckc[mp[0]]) * scale_c for mp in maps}
            m = {mp: jnp.maximum(m[mp], jnp.max(sc[mp], axis=1, keepdims=True)) for mp in maps}
        e = {mp: jnp.exp(s[mp] - m[mp]) for mp in maps}
        l = {mp: jnp.sum(e[mp], axis=1, keepdims=True) for mp in maps}
        if latent:
            ec = {mp: jnp.exp(sc[mp] - m[mp]) for mp in maps}
            l = {mp: l[mp] + jnp.sum(ec[mp], axis=1, keepdims=True) for mp in maps}
        rl = {mp: 1.0 / l[mp] for mp in maps}
        a_loc = {ph: e[ph + (0,)] * rl[ph + (0,)] - lam * (e[ph + (1,)] * rl[ph + (1,)]) for ph in heads}
        o = {ph: _dot(a_loc[ph].astype(BF16), vcb[ph[0]]) for ph in heads}
        if latent:
            a_ctx = {ph: ec[ph + (0,)] * rl[ph + (0,)] - lam * (ec[ph + (1,)] * rl[ph + (1,)]) for ph in heads}
            o = {ph: o[ph] + _dot(a_ctx[ph].astype(BF16), cvc[ph[0]]) for ph in heads}
        for p in ps:
            ocat = jnp.where(lo_q, o[(p, 0)], o[(p, 1)])
            mix_ref[:, 768 + LANES * p:768 + LANES * (p + 1)] = (
                _rms(ocat, pv_ref[PV_CO:PV_CO + 1, :], half) * (1.0 - lam_init))


def _mixer_scratch(T):
    return [pltpu.VMEM((T, LANES), F32), pltpu.VMEM((LANES, T), F32), pltpu.VMEM((LANES, T), F32),
            pltpu.VMEM((T, LANES), F32), pltpu.VMEM((2, T, LANES), F32)]


def _mixer_ctx(z, pv, lam_init):
    T = SEQ
    body = functools.partial(_mixer_body, latent=False, tq=T, T=T, lam_init=lam_init)
    per_b = lambda shape: pl.BlockSpec((None,) + shape, lambda b, qi: (b,) + (0,) * len(shape))
    return pl.pallas_call(
        body,
        grid=(BATCH, 1),
        in_specs=[pl.BlockSpec((T, Z_COLS), lambda b, qi: (b, 0)),
                  pl.BlockSpec((PV_ROWS, LANES), lambda b, qi: (0, 0))],
        out_specs=[pl.BlockSpec((T, D_MODEL), lambda b, qi: (b, 0)),
                   per_b((T, LANES)), per_b((T, LANES)), per_b((T, 2 * LANES)), per_b((T, 2 * LANES)),
                   per_b((2, 2, LANES, LANES)), per_b((4, LANES)), per_b((1, LANES))],
        out_shape=[SDS((N_CTX, D_MODEL), F32),
                   SDS((BATCH, T, LANES), F32), SDS((BATCH, T, LANES), F32),
                   SDS((BATCH, T, 2 * LANES), F32), SDS((BATCH, T, 2 * LANES), F32),
                   SDS((BATCH, 2, 2, LANES, LANES), F32), SDS((BATCH, 4, LANES), F32), SDS((BATCH, 1, LANES), F32)],
        scratch_shapes=_mixer_scratch(T),
        compiler_params=pltpu.CompilerParams(vmem_limit_bytes=VMEM_LIMIT,
                                             dimension_semantics=("arbitrary", "arbitrary")),
        name="mixer_ctx",
    )(z, pv)


def _mixer_lat(z, pv, rope, ckb, cvb, ckc, cvc, cbd, n0, m0, layer, lam_init):
    T = DEC_SEQ
    tq = TQ_LAT
    P = PAST_LEN
    body = functools.partial(_mixer_body, latent=True, tq=tq, T=T, lam_init=lam_init)
    ctx_off = N_CTX // T
    cache = lambda shape: pl.BlockSpec((None, None) + shape, lambda b, qi: (b, layer) + (0,) * len(shape))
    return pl.pallas_call(
        body,
        grid=(DEC_BATCH, T // tq),
        in_specs=[pl.BlockSpec((T, Z_COLS), lambda b, qi: (ctx_off + b, 0)),
                  pl.BlockSpec((PV_ROWS, LANES), lambda b, qi: (0, 0)),
                  pl.BlockSpec((4, T, LANES), lambda b, qi: (0, 0, 0)),
                  cache((P, LANES)), cache((P, LANES)), cache((P, 2 * LANES)), cache((P, 2 * LANES)),
                  cache((2, 2, LANES, LANES)), cache((4, LANES)), cache((1, LANES))],
        out_specs=pl.BlockSpec((tq, D_MODEL), lambda b, qi: (b * (T // tq) + qi, 0)),
        out_shape=SDS((N_LAT, D_MODEL), F32),
        scratch_shapes=_mixer_scratch(T),
        compiler_params=pltpu.CompilerParams(vmem_limit_bytes=VMEM_LIMIT,
                                             dimension_semantics=("arbitrary", "arbitrary")),
        name="mixer_lat",
    )(z, pv, rope, ckb, cvb, ckc, cvc, cbd, n0, m0)


def _out_body(x_ref, mixc_ref, mixl_ref, mod_ref, g_ref, wo_ref, xo_ref, h2_ref):
    from_ctx = pl.program_id(0) < N_CTX // TM_TOK
    mix = jnp.where(from_ctx, mixc_ref[...], mixl_ref[...])
    y = _dot(mix.astype(BF16), wo_ref[...])
    x = x_ref[...] + mod_ref[2:3, :] * y
    xo_ref[...] = x
    h2_ref[...] = _norm_mod(x, g_ref[...], mod_ref[4:5, :], mod_ref[3:4, :]).astype(BF16)


def _out_proj(x, mix_c, mix_l, mod, gain, w_out, layer):
    tm = TM_TOK
    n_ctx_tiles = N_CTX // tm
    return pl.pallas_call(
        _out_body,
        grid=(N_TOK // tm,),
        in_specs=[pl.BlockSpec((tm, D_MODEL), lambda i: (i, 0)),
                  pl.BlockSpec((tm, D_MODEL), lambda i: (jnp.minimum(i, n_ctx_tiles - 1), 0)),
                  pl.BlockSpec((tm, D_MODEL), lambda i: (jnp.maximum(i - n_ctx_tiles, 0), 0)),
                  pl.BlockSpec((None, 6, D_MODEL), lambda i: (_group_of_tile(i, tm), 0, 0)),
                  pl.BlockSpec((1, D_MODEL), lambda i: (0, 0)),
                  pl.BlockSpec((None, D_MODEL, D_MODEL), lambda i: (layer, 0, 0))],
        out_specs=[pl.BlockSpec((tm, D_MODEL), lambda i: (i, 0)),
                   pl.BlockSpec((tm, D_MODEL), lambda i: (i, 0))],
        out_shape=[SDS((N_TOK, D_MODEL), F32), SDS((N_TOK, D_MODEL), BF16)],
        compiler_params=pltpu.CompilerParams(vmem_limit_bytes=VMEM_LIMIT),
        name="out_proj",
    )(x, mix_c, mix_l, mod, gain, w_out)


def _take_top16(s):
    n_rows = s.shape[0]
    ridx = _iota(s.shape, 0)
    rank = jnp.full(s.shape, PEER_TOPK, jnp.int32)
    vals = []
    for a in range(PEER_TOPK):
        mx = jnp.max(s, axis=0, keepdims=True)
        idx = jnp.min(jnp.where(s == mx, ridx, n_rows), axis=0, keepdims=True)
        hit = ridx == idx
        rank = jnp.where(hit, a, rank)
        s = jnp.where(hit, -jnp.inf, s)
        vals.append(mx)
    return vals, rank


_CAND_WIDTH = (16, 8, 5, 4, 3, 2, 2, 2)


def _sort16_pairs():
    n, pairs, p = 16, [], 1
    while p < n:
        k = p
        while k >= 1:
            for j in range(k % p, n - k, 2 * k):
                for i in range(min(k, n - j - k)):
                    if (i + j) // (2 * p) == (i + j + k) // (2 * p):
                        pairs.append((i + j, i + j + k))
            k //= 2
        p *= 2
    return pairs


_SORT16 = _sort16_pairs()
SUBLANES = 8


def _sublane_sum(x):
    for d in (4, 2, 1):
        x = x + pltpu.roll(x, d, axis=0)
    return x


def _top16_values(groups):
    g = list(groups)

    def exchange(i, j):
        if g[j] is None:
            return
        if g[i] is None:
            g[i], g[j] = g[j], None
            return
        g[i], g[j] = jnp.maximum(g[i], g[j]), jnp.minimum(g[i], g[j])

    for i, j in _SORT16:
        exchange(i, j)
    for d in (4, 2, 1):
        merged = []
        for v in range(16):
            other = g[15 - v]
            if other is None:
                merged.append(g[v])
            elif g[v] is None:
                merged.append(pltpu.roll(other, d, axis=0))
            else:
                merged.append(jnp.maximum(g[v], pltpu.roll(other, d, axis=0)))
        g = merged
        for stride in (8, 4, 2, 1):
            for i in range(16):
                if (i & stride) == 0:
                    exchange(i, i + stride)
    return g


def _topk_fast(s1, s2, a_ref, cnt_ref, b_ref, r2_ref):
    tl = s1.shape[1]
    g1 = [s1[SUBLANES * v:SUBLANES * (v + 1)] for v in range(16)]
    g2 = [s2[SUBLANES * v:SUBLANES * (v + 1)] for v in range(16)]
    v1 = _top16_values(g1)
    v2 = _top16_values(g2)
    sub = _iota((SUBLANES, tl), 0)
    v2lo, v2hi, v1hi = v2[7], v2[15], v1[15]
    for b in range(6, -1, -1):
        v2lo = jnp.where(sub == b, v2[b], v2lo)
        v2hi = jnp.where(sub == b, v2[8 + b], v2hi)
        v1hi = jnp.where(sub == b, v1[8 + b], v1hi)
    cands = [v1[0] + v2lo, v1[0] + v2hi, v1[1] + v2lo]
    for a in range(2, 8):
        cands.append(jnp.where(sub < _CAND_WIDTH[a], v1[a] + v2lo, -jnp.inf))
    cands.append(v1hi + v2[0])
    tau = _top16_values(cands + [None] * 6)[15]
    tmax = v1[0] + v2[0]
    sel = [(c >= tau).astype(F32) for c in cands]
    cnt = [_sublane_sum(sel[0] + sel[1]), _sublane_sum(sel[2])]
    for a in range(2, 8):
        cnt.append(_sublane_sum(sel[a + 1]))
    for a in range(8, 16):
        cnt.append(((v1[a] + v2[0]) >= tau).astype(F32))
    z = sel[0] * jnp.exp(cands[0] - tmax)
    for k in range(1, len(cands)):
        z = z + sel[k] * jnp.exp(cands[k] - tmax)
    inv_z = 1.0 / _sublane_sum(z)
    total = cnt[0]
    for a in range(1, 16):
        total = total + cnt[a]
    n1 = jnp.zeros((SUBLANES, tl), F32)
    n2 = jnp.zeros((SUBLANES, tl), F32)
    for v in range(16):
        rows = slice(SUBLANES * v, SUBLANES * (v + 1))
        c1 = jnp.zeros((SUBLANES, tl), F32)
        r2 = jnp.full((SUBLANES, tl), float(PEER_TOPK), F32)
        for a in range(PEER_TOPK - 1, -1, -1):
            c1 = jnp.where(g1[v] >= v1[a], cnt[a], c1)
            r2 = jnp.where(g2[v] >= v2[a], float(a), r2)
        n1 = n1 + (g1[v] >= v1[15]).astype(F32)
        n2 = n2 + (g2[v] >= v2[15]).astype(F32)
        a_ref[rows, :] = jnp.exp(g1[v] - v1[0])
        cnt_ref[rows, :] = c1
        b_ref[rows, :] = (jnp.exp(g2[v] - v2[0]) * inv_z).astype(BF16)
        r2_ref[rows, :] = r2.astype(BF16)
    bad = (total != float(PEER_TOPK))
    bad = bad | ((_sublane_sum(n1) != float(PEER_TOPK)) & (cnt[15] > 0.0))
    bad = bad | ((_sublane_sum(n2) != float(PEER_TOPK)) & (cnt[0] >= float(PEER_TOPK)))
    return bad.astype(F32)


def _topk_body(h2_ref, wq_ref, sk_ref, a_ref, cnt_ref, b_ref, r2_ref):
    qT = _dot_nt(wq_ref[...], h2_ref[...]).astype(BF16)
    s1 = _dot(sk_ref[0], qT[0:N_KEYS])
    s2 = _dot(sk_ref[1], qT[N_KEYS:2 * N_KEYS])
    bad = _topk_fast(s1, s2, a_ref, cnt_ref, b_ref, r2_ref)

    @pl.when(jnp.max(bad) > 0.0)
    def _():
        _topk_exact(s1, s2, a_ref, cnt_ref, b_ref, r2_ref)


def _topk_exact(s1, s2, a_ref, cnt_ref, b_ref, r2_ref):
    tl = s1.shape[1]
    v1, rank1 = _take_top16(s1)
    v2, rank2 = _take_top16(s2)
    V1 = jnp.concatenate(v1, axis=0)
    V2 = jnp.concatenate(v2, axis=0)
    b8 = _iota((8, tl), 0)
    pieces = [v1[0] + V2, v1[1] + V2[0:8]]
    for a in range(2, 8):
        pieces.append(jnp.where(b8 < _CAND_WIDTH[a], v1[a] + V2[0:8], -jnp.inf))
    pieces.append(V1[8:16] + v2[0])
    cand = jnp.concatenate(pieces, axis=0)
    n_c = cand.shape[0]
    cidx = _iota(cand.shape, 0)
    work = cand
    sel = jnp.zeros(cand.shape, F32)
    for _ in range(PEER_TOPK):
        mx = jnp.max(work, axis=0, keepdims=True)
        idx = jnp.min(jnp.where(work == mx, cidx, n_c), axis=0, keepdims=True)
        hit = cidx == idx
        sel = jnp.where(hit, 1.0, sel)
        work = jnp.where(hit, -jnp.inf, work)
    tmax = v1[0] + v2[0]
    z = jnp.sum(sel * jnp.exp(jnp.where(sel > 0.0, cand, tmax) - tmax), axis=0, keepdims=True)
    cnt = [jnp.sum(sel[0:16], axis=0, keepdims=True), jnp.sum(sel[16:24], axis=0, keepdims=True)]
    for a in range(2, 8):
        cnt.append(jnp.sum(sel[8 * a + 8:8 * a + 16], axis=0, keepdims=True))
    for a in range(8, 16):
        cnt.append(sel[64 + a:65 + a])
    cnt1 = jnp.zeros(s1.shape, F32)
    for a in range(PEER_TOPK):
        cnt1 = jnp.where(rank1 == a, cnt[a], cnt1)
    a_ref[...] = jnp.exp(s1 - v1[0])
    cnt_ref[...] = cnt1
    b_ref[...] = (jnp.exp(s2 - v2[0]) * (1.0 / z)).astype(BF16)
    r2_ref[...] = rank2.astype(F32).astype(BF16)


def _peer_topk(h2, wq_t, subkeys, layer):
    tl = TL_TOPK
    outs = SDS((N_TOK // tl, PEER_HEADS, N_KEYS, tl), F32)
    outs_b = SDS((N_TOK // tl, PEER_HEADS, N_KEYS, tl), BF16)
    ospec = pl.BlockSpec((None, None, N_KEYS, tl), lambda i, h: (i, h, 0, 0))
    return pl.pallas_call(
        _topk_body,
        grid=(N_TOK // tl, PEER_HEADS),
        in_specs=[pl.BlockSpec((tl, D_MODEL), lambda i, h: (i, 0)),
                  pl.BlockSpec((None, 2 * N_KEYS, D_MODEL), lambda i, h: (layer, h, 0)),
                  pl.BlockSpec((None, None, 2, N_KEYS, N_KEYS), lambda i, h: (layer, h, 0, 0, 0))],
        out_specs=[ospec] * 4,
        out_shape=[outs, outs, outs_b, outs_b],
        compiler_params=pltpu.CompilerParams(vmem_limit_bytes=VMEM_LIMIT),
        name="peer_topk",
    )(h2, wq_t, subkeys)


def _peer_step(cur, s, h2_ref, a_ref, cnt_ref, b_ref, r2_ref, u_ref, vt_ref, acc_sc, act_sc, hw_sc):
    prev = 1 - cur
    n_blocks = pl.num_programs(1) - 2
    nt = h2_ref.shape[0]
    valid = jnp.logical_and(s >= 1, s <= n_blocks).astype(F32)
    blk = jnp.clip(s - 1, 0, n_blocks - 1)
    rows_per_step = EB_PEER // N_KEYS

    for lc in range(nt // TL_TOPK):
        tc = slice(TL_TOPK * lc, TL_TOPK * (lc + 1))
        cnt_rows = [[(cnt_ref[lc, h, pl.ds(blk * rows_per_step + ii, 1), :] * valid).astype(BF16)
                     for h in range(PEER_HEADS)] for ii in range(rows_per_step)]
        a_rows = [[a_ref[lc, h, pl.ds(blk * rows_per_step + ii, 1), :].astype(BF16)
                   for h in range(PEER_HEADS)] for ii in range(rows_per_step)]
        acc_sc[:, tc] += _dot(vt_ref[...], hw_sc[cur, :, tc])
        for ii in range(rows_per_step):
            er = slice(N_KEYS * ii, N_KEYS * (ii + 1))
            gate = jnp.zeros((N_KEYS, TL_TOPK), BF16)
            for h in range(PEER_HEADS):
                gate = gate + jnp.where(r2_ref[lc, h] < cnt_rows[ii][h], b_ref[lc, h] * a_rows[ii][h],
                                        jnp.zeros((), BF16))
            act = act_sc[prev, er, tc]
            gel = 0.5 * act * (1.0 + lax.erf(act * math.sqrt(0.5)))
            hw_sc[prev, er, tc] = gel.astype(BF16) * gate
        act_sc[cur, :, tc] = _dot_nt(u_ref[...], h2_ref[tc, :])


def _peer_body(x_ref, h2_ref, mod_ref, a_ref, cnt_ref, b_ref, r2_ref, u_ref, vt_ref, o_ref, acc_sc, act_sc, hw_sc):
    s = pl.program_id(1)

    @pl.when(s == 0)
    def _():
        acc_sc[...] = jnp.zeros_like(acc_sc)
        act_sc[...] = jnp.zeros_like(act_sc)
        hw_sc[...] = jnp.zeros_like(hw_sc)

    _peer_step(s % 2, s, h2_ref, a_ref, cnt_ref, b_ref, r2_ref, u_ref, vt_ref, acc_sc, act_sc, hw_sc)

    @pl.when(s == pl.num_programs(1) - 1)
    def _():
        o_ref[...] = x_ref[...] + mod_ref[5:6, :] * acc_sc[...].T


def _peer_dense(x, h2, mod, at, cntt, bt, r2t, u, vt, layer):
    nt = NT_PEER
    eb = EB_PEER
    n_blocks = N_EXPERTS // eb
    fac = pl.BlockSpec((nt // TL_TOPK, PEER_HEADS, N_KEYS, TL_TOPK), lambda i, s: (i, 0, 0, 0))
    return pl.pallas_call(
        _peer_body,
        grid=(N_TOK // nt, n_blocks + 2),
        in_specs=[pl.BlockSpec((nt, D_MODEL), lambda i, s: (i, 0)),
                  pl.BlockSpec((nt, D_MODEL), lambda i, s: (i, 0)),
                  pl.BlockSpec((None, 6, D_MODEL), lambda i, s: (_group_of_tile(i, nt), 0, 0)),
                  fac, fac, fac, fac,
                  pl.BlockSpec((None, eb, D_MODEL), lambda i, s: (layer, jnp.minimum(s, n_blocks - 1), 0)),
                  pl.BlockSpec((None, None, D_MODEL, eb), lambda i, s: (layer, jnp.maximum(s - 2, 0), 0, 0))],
        out_specs=pl.BlockSpec((nt, D_MODEL), lambda i, s: (i, 0)),
        out_shape=SDS((N_TOK, D_MODEL), F32),
        scratch_shapes=[pltpu.VMEM((D_MODEL, nt), F32), pltpu.VMEM((2, eb, nt), F32),
                        pltpu.VMEM((2, eb, nt), BF16)],
        compiler_params=pltpu.CompilerParams(vmem_limit_bytes=VMEM_LIMIT,
                                             dimension_semantics=("arbitrary", "arbitrary")),
        name="peer_dense",
    )(x, h2, mod, at, cntt, bt, r2t, u, vt)


def _rope_tables():
    t = jnp.arange(DEC_SEQ)
    row = (t // GRID_W).astype(F32)[:, None]
    col = (t % GRID_W).astype(F32)[:, None]
    lane = jnp.arange(LANES)

    def tables(dim):
        quarter = dim // 4
        inv = ROPE_BASE ** (-jnp.arange(quarter, dtype=F32) / quarter)
        d = lane % dim
        use_col = (d // (dim // 2)) == 1
        e = d % (dim // 2)
        ang = jnp.where(use_col[None, :], col * inv[e % quarter][None, :], row * inv[e % quarter][None, :])
        sign = jnp.where(e < quarter, -1.0, 1.0).astype(F32)[None, :]
        return jnp.cos(ang), jnp.sin(ang) * sign

    cb, sb = tables(B_HD)
    cc, sc = tables(C_DH)
    return jnp.stack([cb, sb, cc, sc], axis=0)


def _pack_vectors(l, a_out_gain, b_q_gain, b_k_gain, b_sink, c_q_gain, c_k_gain, c_lambda, c_out_gain):
    pad = lambda v: jnp.pad(v, (0, LANES - v.shape[0]))
    rows = [jnp.tile(b_q_gain[l], 2), jnp.tile(b_k_gain[l], 2),
            jnp.tile(c_q_gain[l].reshape(-1), 2), jnp.tile(c_k_gain[l].reshape(-1), 2),
            a_out_gain[l, :LANES], a_out_gain[l, LANES:],
            jnp.tile(c_out_gain[l], 2), pad(b_sink[l])]
    rows += [pad(c_lambda[l, r]) for r in range(4)]
    rows += [jnp.zeros((LANES,), F32)] * (PV_ROWS - len(rows))
    return jnp.stack(rows, axis=0).astype(F32)


def kernel(x_prompt, x_sample, cache_swa_k, cache_swa_v, cache_diff_k, cache_diff_v, state_mlstm_C, state_mlstm_n,
           state_mlstm_m, c, c_ctx, w_ada, b_ada, norm_gain, w_in, b_in, a_out_gain, b_q_gain, b_k_gain, b_sink,
           c_q_gain, c_k_gain, c_lambda, c_out_gain, w_out, peer_wq, peer_subkeys, peer_u, peer_v):
    P = PAST_LEN
    x = jnp.concatenate([x_prompt.reshape(N_CTX, D_MODEL), x_sample.reshape(N_LAT, D_MODEL)], axis=0)
    cond8 = jnp.concatenate([c_ctx[None, :], c, jnp.zeros((5, D_MODEL), F32)], axis=0)
    mod_all = _ada_all(cond8, w_ada, b_ada)[:, :3].reshape(DEPTH, 3, 6, D_MODEL)

    gate_pad = LANES - N_GATES
    w_in_r = jnp.concatenate([w_in[:, :, :GATE_SRC], w_in[:, :, GATE_SRC + N_GATES:],
                              w_in[:, :, GATE_SRC:GATE_SRC + N_GATES],
                              jnp.zeros((DEPTH, D_MODEL, gate_pad), F32)], axis=-1).astype(BF16)
    b_in_r = jnp.concatenate([b_in[:, :GATE_SRC], b_in[:, GATE_SRC + N_GATES:],
                              b_in[:, GATE_SRC:GATE_SRC + N_GATES],
                              jnp.zeros((DEPTH, gate_pad), F32)], axis=-1).reshape(DEPTH, 1, Z_COLS)
    w_out_b = w_out.astype(BF16)
    wq_t = jnp.swapaxes(peer_wq, 1, 2).astype(BF16)
    subkeys_b = peer_subkeys.astype(BF16)
    u_b = peer_u.astype(BF16)
    vt_b = jnp.swapaxes(peer_v.astype(BF16).reshape(DEPTH, N_EXPERTS // EB_PEER, EB_PEER, D_MODEL), 2, 3)

    rope = _rope_tables()
    ckb = cache_swa_k.reshape(DEC_BATCH, DEPTH, P, LANES)
    cvb = cache_swa_v.reshape(DEC_BATCH, DEPTH, P, LANES)
    ckc = cache_diff_k.reshape(DEC_BATCH, DEPTH, P, 2 * LANES)
    cvc = cache_diff_v.reshape(DEC_BATCH, DEPTH, P, 2 * LANES)
    sc6 = state_mlstm_C.reshape(DEC_BATCH, DEPTH, 2, 2, 2, A_DK, A_DK)
    zero = jnp.zeros_like(sc6[..., 0, :, :])
    cbd = jnp.concatenate([jnp.concatenate([sc6[..., 0, :, :], zero], axis=-1),
                           jnp.concatenate([zero, sc6[..., 1, :, :]], axis=-1)], axis=-2)
    n0 = state_mlstm_n.reshape(DEC_BATCH, DEPTH, 4, LANES)
    sm = state_mlstm_m
    m0 = jnp.zeros((DEC_BATCH, DEPTH, 1, LANES), F32)
    m0 = m0.at[:, :, 0, 4:8].set(sm[:, :, 0]).at[:, :, 0, 12:16].set(sm[:, :, 1])

    ctx = [[] for _ in range(7)]
    for l in range(DEPTH):
        lam_init = 0.8 - 0.6 * math.exp(-0.3 * l)
        pv = _pack_vectors(l, a_out_gain, b_q_gain, b_k_gain, b_sink, c_q_gain, c_k_gain, c_lambda, c_out_gain)
        z = _in_proj(x, mod_all[l], norm_gain[l, 0:1], w_in_r, b_in_r[l], l)
        mix_c, kb, vb, kc, vc, cst, nst, mst = _mixer_ctx(z, pv, lam_init)
        mix_l = _mixer_lat(z, pv, rope, ckb, cvb, ckc, cvc, cbd, n0, m0, l, lam_init)
        x, h2 = _out_proj(x, mix_c, mix_l, mod_all[l], norm_gain[l, 1:2], w_out_b, l)
        at, cntt, bt, r2t = _peer_topk(h2, wq_t, subkeys_b, l)
        x = _peer_dense(x, h2, mod_all[l], at, cntt, bt, r2t, u_b, vt_b, l)
        c4 = jnp.stack([cst[:, :, :, :A_DK, :A_DK], cst[:, :, :, A_DK:, A_DK:]], axis=3)
        for j, v in enumerate((kb, vb, kc, vc, c4.reshape(BATCH, 2, 4, A_DK, A_DK),
                               nst.reshape(BATCH, 2, 4, A_DK),
                               jnp.stack([mst[:, 0, 4:8], mst[:, 0, 12:16]], axis=1))):
            ctx[j].append(v)

    stack = lambda j: jnp.stack(ctx[j], axis=1)
    yp = x[:N_CTX].reshape(BATCH, SEQ, D_MODEL)
    ys = x[N_CTX:].reshape(DEC_BATCH, DEC_SEQ, D_MODEL)
    return (yp, ys,
            stack(0).reshape(BATCH, DEPTH, SEQ, 2, B_HD), stack(1).reshape(BATCH, DEPTH, SEQ, 2, B_HD),
            stack(2).reshape(BATCH, DEPTH, SEQ, 4, 2 * C_DH), stack(3).reshape(BATCH, DEPTH, SEQ, 4, 64),
            stack(4), stack(5), stack(6))
```

```python
import functools
import math

import jax
import jax.numpy as jnp
from jax import lax
from jax.experimental import pallas as pl
from jax.experimental.pallas import tpu as pltpu

F32 = jnp.float32
BF16 = jnp.bfloat16
HIGHEST = lax.Precision.HIGHEST
SDS = jax.ShapeDtypeStruct

D_MODEL = 1024
BATCH = 16
SEQ = 256
DEPTH = 4
DEC_BATCH = 2
DEC_SEQ = 1024
PAST_LEN = 256
GRID_W = 64
WINDOW = 128
ROPE_BASE = 10000.0
EPS = 1e-6
N_KEYS = 128
N_EXPERTS = N_KEYS * N_KEYS
PEER_HEADS = 8
PEER_TOPK = 16
A_DK = 64
B_HD = 64
C_DH = 32

LANES = 128
N_CTX = BATCH * SEQ
N_LAT = DEC_BATCH * DEC_SEQ
N_TOK = N_CTX + N_LAT

COL_AQ, COL_AK, COL_AV, COL_AO = 0, 256, 512, 768
COL_BQ, COL_BK, COL_BV = 1024, 1536, 1664
COL_CQ, COL_CK, COL_CV = 1792, 2048, 2304
COL_AG = 2560
Z_COLS = 2688
GATE_SRC = 1024
N_GATES = 16

PV_BQ, PV_BK, PV_CQ, PV_CK, PV_AO, PV_CO, PV_SINK, PV_LAM = 0, 1, 2, 3, 4, 6, 7, 8
PV_ROWS = 16

TM_TOK = 512
TQ_LAT = 256
TL_TOPK = 256
NT_PEER = 512
EB_PEER = 512
VMEM_LIMIT = 52 * 1024 * 1024

def _iota(shape, dim):
    return lax.broadcasted_iota(jnp.int32, shape, dim)


def _dot_nt(a, b):
    return lax.dot_general(a, b, (((1,), (1,)), ((), ())), preferred_element_type=F32)


def _dot(a, b):
    return jnp.dot(a, b, preferred_element_type=F32)


def _group_of_tile(i, tm):
    n_ctx_tiles = N_CTX // tm
    per_batch = DEC_SEQ // tm
    return jnp.where(i < n_ctx_tiles, 0, 1 + (i - n_ctx_tiles) // per_batch)


def _norm_mod(x, gain, scale, shift):
    y = x * lax.rsqrt(jnp.mean(x * x, axis=-1, keepdims=True) + EPS)
    return (y * gain) * (1.0 + scale) + shift


def _ada_body(c_ref, w_ref, b_ref, o_ref):
    c = c_ref[...]
    a = c * jax.nn.sigmoid(c)
    o_ref[...] = jnp.dot(a, w_ref[...], precision=HIGHEST, preferred_element_type=F32) + b_ref[...]


def _ada_all(cond8, w_ada, b_ada):
    tn = 1536
    return pl.pallas_call(
        _ada_body,
        grid=(DEPTH, 6 * D_MODEL // tn),
        in_specs=[pl.BlockSpec((8, D_MODEL), lambda l, j: (0, 0)),
                  pl.BlockSpec((None, D_MODEL, tn), lambda l, j: (l, 0, j)),
                  pl.BlockSpec((None, 1, tn), lambda l, j: (l, 0, j))],
        out_specs=pl.BlockSpec((None, 8, tn), lambda l, j: (l, 0, j)),
        out_shape=SDS((DEPTH, 8, 6 * D_MODEL), F32),
        compiler_params=pltpu.CompilerParams(vmem_limit_bytes=VMEM_LIMIT),
        name="ada_mod",
    )(cond8, w_ada, b_ada.reshape(DEPTH, 1, 6 * D_MODEL))


def _in_body(x_ref, mod_ref, g_ref, w_ref, b_ref, z_ref):
    h = _norm_mod(x_ref[...], g_ref[...], mod_ref[1:2, :], mod_ref[0:1, :])
    z_ref[...] = _dot(h.astype(BF16), w_ref[...]) + b_ref[...]


def _in_proj(x, mod, gain, w, b, layer):
    tm = TM_TOK
    return pl.pallas_call(
        _in_body,
        grid=(N_TOK // tm,),
        in_specs=[pl.BlockSpec((tm, D_MODEL), lambda i: (i, 0)),
                  pl.BlockSpec((None, 6, D_MODEL), lambda i: (_group_of_tile(i, tm), 0, 0)),
                  pl.BlockSpec((1, D_MODEL), lambda i: (0, 0)),
                  pl.BlockSpec((None, D_MODEL, Z_COLS), lambda i: (layer, 0, 0)),
                  pl.BlockSpec((1, Z_COLS), lambda i: (0, 0))],
        out_specs=pl.BlockSpec((tm, Z_COLS), lambda i: (i, 0)),
        out_shape=SDS((N_TOK, Z_COLS), F32),
        compiler_params=pltpu.CompilerParams(vmem_limit_bytes=VMEM_LIMIT),
        name="in_proj",
    )(x, mod, gain, w, b)


def _block_mean_sq(x, blk):
    r = _iota((LANES, LANES), 0) // blk
    c = _iota((LANES, LANES), 1) // blk
    ones = jnp.where(r == c, 1.0 / blk, 0.0).astype(F32)
    return jnp.dot(x * x, ones, precision=HIGHEST, preferred_element_type=F32)


def _rms(x, gain, blk):
    return x * lax.rsqrt(_block_mean_sq(x, blk) + EPS) * gain


def _rope(x, cos, sin_signed, quarter):
    lane = _iota(x.shape, 1)
    first = (lane % (2 * quarter)) < quarter
    partner = jnp.where(first, pltpu.roll(x, LANES - quarter, axis=1), pltpu.roll(x, quarter, axis=1))
    return x * cos + partner * sin_signed


def _dup_half(x, g):
    lane = _iota(x.shape, 1)
    r = pltpu.roll(x, LANES // 2, axis=1)
    return jnp.where(lane < LANES // 2, x, r) if g == 0 else jnp.where(lane < LANES // 2, r, x)


def _lane_block_mask(shape, start, width):
    lane = _iota(shape, 1)
    return (lane >= start) & (lane < start + width)


def _mixer_body(*refs, latent, tq, T, lam_init):
    if latent:
        (z_ref, pv_ref, rope_ref, ckb_ref, cvb_ref, ckc_ref, cvc_ref, cbd_ref, n0_ref, m0_ref,
         mix_ref, bsel_sc, bselT_sc, zgT_sc, kb_sc, kc_sc) = refs
    else:
        (z_ref, pv_ref, mix_ref, kb_ref, vb_ref, kc_ref, vc_ref, cst_ref, nst_ref, mst_ref,
         bsel_sc, bselT_sc, zgT_sc, kb_sc, kc_sc) = refs

    qi = pl.program_id(1)
    r0 = pl.multiple_of(qi * tq, tq)
    rows = pl.ds(r0, tq)
    half = LANES // 2

    @pl.when(qi == 0)
    def _():
        zg = z_ref[:, COL_AG:COL_AG + LANES]
        ls = jax.nn.log_sigmoid(zg)
        rr = _iota((LANES, LANES), 0)
        cc = _iota((LANES, LANES), 1)
        tril = (rr >= cc).astype(F32)
        triu = (rr <= cc).astype(F32)
        n_blk = T // LANES
        pre = [jnp.dot(tril, ls[LANES * k:LANES * (k + 1)], precision=HIGHEST, preferred_element_type=F32)
               for k in range(n_blk)]
        suf = [jnp.dot(triu, ls[LANES * k:LANES * (k + 1)], precision=HIGHEST, preferred_element_type=F32)
               for k in range(n_blk)]
        tot = [pre[k][LANES - 1:LANES, :] for k in range(n_blk)]
        fwd_lane = _iota((LANES, LANES), 1) < 8
        off_f = jnp.zeros((1, LANES), F32)
        off_b = [jnp.zeros((1, LANES), F32)] * n_blk
        for k in range(n_blk - 2, -1, -1):
            off_b[k] = off_b[k + 1] + tot[k + 1]
        for k in range(n_blk):
            bsel_sc[LANES * k:LANES * (k + 1), :] = jnp.where(fwd_lane, pre[k] + off_f, suf[k] + off_b[k])
            off_f = off_f + tot[k]
        bselT_sc[...] = bsel_sc[...].T
        zgT_sc[...] = zg.T
        kb = _rms(z_ref[:, COL_BK:COL_BK + LANES], pv_ref[PV_BK:PV_BK + 1, :], B_HD)
        if latent:
            kb = _rope(kb, rope_ref[0], rope_ref[1], B_HD // 4)
        kb_sc[...] = kb
        for p in range(2):
            kc = _rms(z_ref[:, COL_CK + LANES * p:COL_CK + LANES * (p + 1)], pv_ref[PV_CK:PV_CK + 1, :], C_DH)
            if latent:
                kc = _rope(kc, rope_ref[2], rope_ref[3], C_DH // 4)
            kc_sc[p] = kc

    lane_q = _iota((tq, LANES), 1)
    lo_q = lane_q < half
    hmasks = (lo_q, jnp.logical_not(lo_q))
    jq = r0 + _iota((tq, T), 0)
    sk = _iota((tq, T), 1)
    pair_groups = ((0,), (1,)) if latent else ((0, 1),)

    bq = bsel_sc[rows, :]
    bq_m0 = bq + m0_ref[0:1, :] if latent else bq
    causal = ((sk <= jq), (sk >= jq))
    for ps in pair_groups:
        heads = [(p, hh) for p in ps for hh in range(2)]
        chains = [(p, hh, d) for p, hh in heads for d in range(2)]
        q_pair = {p: z_ref[rows, COL_AQ + LANES * p:COL_AQ + LANES * (p + 1)] for p in ps}
        k_pair = {p: (z_ref[:, COL_AK + LANES * p:COL_AK + LANES * (p + 1)] * (A_DK ** -0.5)).astype(BF16) for p in ps}
        v_pair = {p: z_ref[:, COL_AV + LANES * p:COL_AV + LANES * (p + 1)].astype(BF16) for p in ps}
        q_h = {(p, hh): jnp.where(hmasks[hh], q_pair[p], 0.0) for p, hh in heads}
        qk = {ph: _dot_nt(q_h[ph].astype(BF16), k_pair[ph[0]]) for ph in heads}
        col = {(p, hh, d): 4 + 8 * d + 2 * p + hh for p, hh, d in chains}
        dm = {ch: jnp.where(causal[ch[2]], bq[:, col[ch]:col[ch] + 1] - bselT_sc[col[ch]:col[ch] + 1, :]
                            + zgT_sc[col[ch] - 4:col[ch] - 3, :], -jnp.inf) for ch in chains}
        bm = {ch: bq_m0[:, col[ch]:col[ch] + 1] for ch in chains}
        mrow = {ch: jnp.maximum(bm[ch], jnp.max(dm[ch], axis=1, keepdims=True)) for ch in chains}
        s = {ch: qk[ch[:2]] * jnp.exp(dm[ch] - mrow[ch]) for ch in chains}
        den = {ch: jnp.sum(s[ch], axis=1, keepdims=True) for ch in chains}
        if latent:
            inter = {ch: jnp.exp(bm[ch] - mrow[ch]) for ch in chains}
            qn0 = {ch: jnp.sum(q_h[ch[:2]] * n0_ref[2 * ch[2] + ch[0]:2 * ch[2] + ch[0] + 1, :], axis=1, keepdims=True)
                   for ch in chains}
            den = {ch: den[ch] + inter[ch] * qn0[ch] for ch in chains}
            qc0 = {ch: _dot(q_h[ch[:2]].astype(BF16), cbd_ref[ch[2], ch[0]].astype(BF16)) for ch in chains}
        rdn = {ch: 1.0 / jnp.maximum(jnp.abs(den[ch]), jnp.exp(-mrow[ch])) for ch in chains}
        prob = {ph: s[ph + (0,)] * rdn[ph + (0,)] + s[ph + (1,)] * rdn[ph + (1,)] for ph in heads}
        outs = {ph: _dot(prob[ph].astype(BF16), v_pair[ph[0]]) for ph in heads}
        if latent:
            outs = {ph: outs[ph] + (inter[ph + (0,)] * rdn[ph + (0,)]) * qc0[ph + (0,)]
                    + (inter[ph + (1,)] * rdn[ph + (1,)]) * qc0[ph + (1,)] for ph in heads}
        for p in ps:
            hcat = jnp.where(lo_q, outs[(p, 0)], outs[(p, 1)])
            hn = _rms(hcat, pv_ref[PV_AO + p:PV_AO + p + 1, :], half)
            o_pair = z_ref[rows, COL_AO + LANES * p:COL_AO + LANES * (p + 1)]
            mix_ref[:, LANES * p:LANES * (p + 1)] = hn * jax.nn.sigmoid(o_pair)

    if not latent:
        zg = z_ref[:, COL_AG:COL_AG + LANES]
        bsel = bsel_sc[...]
        lane_t = _iota((T, LANES), 1)
        tot = jnp.where(lane_t[0:1, :] < 8, bsel[T - 1:T, :], bsel[0:1, :])
        g = tot - bsel + pltpu.roll(zg, 4, axis=1)
        mfin = jnp.maximum(tot, jnp.max(g, axis=0, keepdims=True))
        w = jnp.exp(g - mfin)
        mst_ref[...] = mfin
        lo_t = lane_t < half
        for d in range(2):
            for p in range(2):
                c0 = 4 + 8 * d + 2 * p
                wsel = jnp.where(lo_t, w[:, c0:c0 + 1], w[:, c0 + 1:c0 + 2])
                k_pair = z_ref[:, COL_AK + LANES * p:COL_AK + LANES * (p + 1)] * (A_DK ** -0.5)
                v_pair = z_ref[:, COL_AV + LANES * p:COL_AV + LANES * (p + 1)]
                kw = k_pair * wsel
                cst_ref[d, p] = lax.dot_general(kw.astype(BF16), v_pair.astype(BF16),
                                                (((0,), (0,)), ((), ())), preferred_element_type=F32)
                nst_ref[2 * d + p:2 * d + p + 1, :] = jnp.sum(kw, axis=0, keepdims=True)

    kb = kb_sc[...]
    vb = z_ref[:, COL_BV:COL_BV + LANES]
    if not latent:
        kb_ref[...] = kb
        vb_ref[...] = vb
    scale_b = B_HD ** -0.5
    if latent:
        k0 = pl.multiple_of(jnp.clip(r0 - WINDOW, 0, T - 2 * tq), LANES)
        kb = kb_sc[pl.ds(k0, 2 * tq), :]
        vb = z_ref[pl.ds(k0, 2 * tq), COL_BV:COL_BV + LANES]
        jq_w = r0 + _iota((tq, 2 * tq), 0)
        sk_w = k0 + _iota((tq, 2 * tq), 1)
        in_window = jnp.abs(jq_w - sk_w) <= WINDOW
    for gs in (((0,), (1,)) if latent else ((0, 1),)):
        heads = [(g, pp, hh) for g in gs for pp in range(2) for hh in range(2)]
        kd = {g: _dup_half(kb, g).astype(BF16) for g in gs}
        vd = {g: _dup_half(vb, g).astype(BF16) for g in gs}
        if latent:
            ckd = {g: _dup_half(ckb_ref[...], g).astype(BF16) for g in gs}
            cvd = {g: _dup_half(cvb_ref[...], g).astype(BF16) for g in gs}
        qn = {}
        for g in gs:
            for pp in range(2):
                p = 2 * g + pp
                q = _rms(z_ref[rows, COL_BQ + LANES * p:COL_BQ + LANES * (p + 1)], pv_ref[PV_BQ:PV_BQ + 1, :], B_HD)
                if latent:
                    q = _rope(q, rope_ref[0, rows, :], rope_ref[1, rows, :], B_HD // 4)
                qn[(g, pp)] = q
        q_h = {h: jnp.where(hmasks[h[2]], qn[h[:2]], 0.0).astype(BF16) for h in heads}
        sink = {h: pv_ref[PV_SINK:PV_SINK + 1, 4 * h[0] + 2 * h[1] + h[2]:4 * h[0] + 2 * h[1] + h[2] + 1] for h in heads}
        s = {h: _dot_nt(q_h[h], kd[h[0]]) * scale_b for h in heads}
        if latent:
            s = {h: jnp.where(in_window, s[h], -jnp.inf) for h in heads}
            sc = {h: _dot_nt(q_h[h], ckd[h[0]]) * scale_b for h in heads}
            m = {h: jnp.maximum(jnp.max(s[h], axis=1, keepdims=True), jnp.max(sc[h], axis=1, keepdims=True))
                 for h in heads}
        else:
            m = {h: jnp.max(s[h], axis=1, keepdims=True) for h in heads}
        m = {h: jnp.maximum(m[h], sink[h]) for h in heads}
        e = {h: jnp.exp(s[h] - m[h]) for h in heads}
        l = {h: jnp.sum(e[h], axis=1, keepdims=True) + jnp.exp(sink[h] - m[h]) for h in heads}
        o = {h: _dot(e[h].astype(BF16), vd[h[0]]) for h in heads}
        if latent:
            ec = {h: jnp.exp(sc[h] - m[h]) for h in heads}
            l = {h: l[h] + jnp.sum(ec[h], axis=1, keepdims=True) for h in heads}
            o = {h: o[h] + _dot(ec[h].astype(BF16), cvd[h[0]]) for h in heads}
        o = {h: o[h] * (1.0 / l[h]) for h in heads}
        for g in gs:
            for pp in range(2):
                p = 2 * g + pp
                mix_ref[:, 256 + LANES * p:256 + LANES * (p + 1)] = jnp.where(lo_q, o[(g, pp, 0)], o[(g, pp, 1)])

    lp = pv_ref[PV_LAM:PV_LAM + 4, :]
    lam = (jnp.exp(jnp.sum(lp[0:1] * lp[1:2], axis=1, keepdims=True))
           - jnp.exp(jnp.sum(lp[2:3] * lp[3:4], axis=1, keepdims=True)) + lam_init)
    scale_c = C_DH ** -0.5
    for ps in pair_groups:
        maps = [(p, hh, c) for p in ps for hh in range(2) for c in range(2)]
        heads = [(p, hh) for p in ps for hh in range(2)]
        kc = {p: kc_sc[p] for p in ps}
        vc = {p: z_ref[:, COL_CV + LANES * p:COL_CV + LANES * (p + 1)] for p in ps}
        if not latent:
            for p in ps:
                kc_ref[:, LANES * p:LANES * (p + 1)] = kc[p]
                vc_ref[:, LANES * p:LANES * (p + 1)] = vc[p]
        kcb = {p: kc[p].astype(BF16) for p in ps}
        vcb = {p: vc[p].astype(BF16) for p in ps}
        if latent:
            ckc = {p: ckc_ref[:, LANES * p:LANES * (p + 1)].astype(BF16) for p in ps}
            cvc = {p: cvc_ref[:, LANES * p:LANES * (p + 1)].astype(BF16) for p in ps}
        qn = {}
        for p in ps:
            q = _rms(z_ref[rows, COL_CQ + LANES * p:COL_CQ + LANES * (p + 1)], pv_ref[PV_CQ:PV_CQ + 1, :], C_DH)
            if latent:
                q = _rope(q, rope_ref[2, rows, :], rope_ref[3, rows, :], C_DH // 4)
            qn[p] = q
        q_m = {mp: jnp.where(_lane_block_mask((tq, LANES), half * mp[1] + C_DH * mp[2], C_DH), qn[mp[0]], 0.0).astype(BF16)
               for mp in maps}
        s = {mp: _dot_nt(q_m[mp], kcb[mp[0]]) * scale_c for mp in maps}
        m = {mp: jnp.max(s[mp], axis=1, keepdims=True) for mp in maps}
        if latent:
            sc = {mp: _dot_nt(q_m[mp], ckc[mp[0]]) * scale_c for mp in maps}
            m = {mp: jnp.maximum(m[mp], jnp.max(sc[mp], axis=1, keepdims=True)) for mp in maps}
        e = {mp: jnp.exp(s[mp] - m[mp]) for mp in maps}
        l = {mp: jnp.sum(e[mp], axis=1, keepdims=True) for mp in maps}
        if latent:
            ec = {mp: jnp.exp(sc[mp] - m[mp]) for mp in maps}
            l = {mp: l[mp] + jnp.sum(ec[mp], axis=1, keepdims=True) for mp in maps}
        rl = {mp: 1.0 / l[mp] for mp in maps}
        a_loc = {ph: e[ph + (0,)] * rl[ph + (0,)] - lam * (e[ph + (1,)] * rl[ph + (1,)]) for ph in heads}
        o = {ph: _dot(a_loc[ph].astype(BF16), vcb[ph[0]]) for ph in heads}
        if latent:
            a_ctx = {ph: ec[ph + (0,)] * rl[ph + (0,)] - lam * (ec[ph + (1,)] * rl[ph + (1,)]) for ph in heads}
            o = {ph: o[ph] + _dot(a_ctx[ph].astype(BF16), cvc[ph[0]]) for ph in heads}
        for p in ps:
            ocat = jnp.where(lo_q, o[(p, 0)], o[(p, 1)])
            mix_ref[:, 768 + LANES * p:768 + LANES * (p + 1)] = (
                _rms(ocat, pv_ref[PV_CO:PV_CO + 1, :], half) * (1.0 - lam_init))


def _mixer_scratch(T):
    return [pltpu.VMEM((T, LANES), F32), pltpu.VMEM((LANES, T), F32), pltpu.VMEM((LANES, T), F32),
            pltpu.VMEM((T, LANES), F32), pltpu.VMEM((2, T, LANES), F32)]


def _mixer_ctx(z, pv, lam_init):
    T = SEQ
    body = functools.partial(_mixer_body, latent=False, tq=T, T=T, lam_init=lam_init)
    per_b = lambda shape: pl.BlockSpec((None,) + shape, lambda b, qi: (b,) + (0,) * len(shape))
    return pl.pallas_call(
        body,
        grid=(BATCH, 1),
        in_specs=[pl.BlockSpec((T, Z_COLS), lambda b, qi: (b, 0)),
                  pl.BlockSpec((PV_ROWS, LANES), lambda b, qi: (0, 0))],
        out_specs=[pl.BlockSpec((T, D_MODEL), lambda b, qi: (b, 0)),
                   per_b((T, LANES)), per_b((T, LANES)), per_b((T, 2 * LANES)), per_b((T, 2 * LANES)),
                   per_b((2, 2, LANES, LANES)), per_b((4, LANES)), per_b((1, LANES))],
        out_shape=[SDS((N_CTX, D_MODEL), F32),
                   SDS((BATCH, T, LANES), F32), SDS((BATCH, T, LANES), F32),
                   SDS((BATCH, T, 2 * LANES), F32), SDS((BATCH, T, 2 * LANES), F32),
                   SDS((BATCH, 2, 2, LANES, LANES), F32), SDS((BATCH, 4, LANES), F32), SDS((BATCH, 1, LANES), F32)],
        scratch_shapes=_mixer_scratch(T),
        compiler_params=pltpu.CompilerParams(vmem_limit_bytes=VMEM_LIMIT,
                                             dimension_semantics=("arbitrary", "arbitrary")),
        name="mixer_ctx",
    )(z, pv)


def _mixer_lat(z, pv, rope, ckb, cvb, ckc, cvc, cbd, n0, m0, layer, lam_init):
    T = DEC_SEQ
    tq = TQ_LAT
    P = PAST_LEN
    body = functools.partial(_mixer_body, latent=True, tq=tq, T=T, lam_init=lam_init)
    ctx_off = N_CTX // T
    cache = lambda shape: pl.BlockSpec((None, None) + shape, lambda b, qi: (b, layer) + (0,) * len(shape))
    return pl.pallas_call(
        body,
        grid=(DEC_BATCH, T // tq),
        in_specs=[pl.BlockSpec((T, Z_COLS), lambda b, qi: (ctx_off + b, 0)),
                  pl.BlockSpec((PV_ROWS, LANES), lambda b, qi: (0, 0)),
                  pl.BlockSpec((4, T, LANES), lambda b, qi: (0, 0, 0)),
                  cache((P, LANES)), cache((P, LANES)), cache((P, 2 * LANES)), cache((P, 2 * LANES)),
                  cache((2, 2, LANES, LANES)), cache((4, LANES)), cache((1, LANES))],
        out_specs=pl.BlockSpec((tq, D_MODEL), lambda b, qi: (b * (T // tq) + qi, 0)),
        out_shape=SDS((N_LAT, D_MODEL), F32),
        scratch_shapes=_mixer_scratch(T),
        compiler_params=pltpu.CompilerParams(vmem_limit_bytes=VMEM_LIMIT,
                                             dimension_semantics=("arbitrary", "arbitrary")),
        name="mixer_lat",
    )(z, pv, rope, ckb, cvb, ckc, cvc, cbd, n0, m0)


def _out_body(x_ref, mixc_ref, mixl_ref, mod_ref, g_ref, wo_ref, xo_ref, h2_ref):
    from_ctx = pl.program_id(0) < N_CTX // TM_TOK
    mix = jnp.where(from_ctx, mixc_ref[...], mixl_ref[...])
    y = _dot(mix.astype(BF16), wo_ref[...])
    x = x_ref[...] + mod_ref[2:3, :] * y
    xo_ref[...] = x
    h2_ref[...] = _norm_mod(x, g_ref[...], mod_ref[4:5, :], mod_ref[3:4, :]).astype(BF16)


def _out_proj(x, mix_c, mix_l, mod, gain, w_out, layer):
    tm = TM_TOK
    n_ctx_tiles = N_CTX // tm
    return pl.pallas_call(
        _out_body,
        grid=(N_TOK // tm,),
        in_specs=[pl.BlockSpec((tm, D_MODEL), lambda i: (i, 0)),
                  pl.BlockSpec((tm, D_MODEL), lambda i: (jnp.minimum(i, n_ctx_tiles - 1), 0)),
                  pl.BlockSpec((tm, D_MODEL), lambda i: (jnp.maximum(i - n_ctx_tiles, 0), 0)),
                  pl.BlockSpec((None, 6, D_MODEL), lambda i: (_group_of_tile(i, tm), 0, 0)),
                  pl.BlockSpec((1, D_MODEL), lambda i: (0, 0)),
                  pl.BlockSpec((None, D_MODEL, D_MODEL), lambda i: (layer, 0, 0))],
        out_specs=[pl.BlockSpec((tm, D_MODEL), lambda i: (i, 0)),
                   pl.BlockSpec((tm, D_MODEL), lambda i: (i, 0))],
        out_shape=[SDS((N_TOK, D_MODEL), F32), SDS((N_TOK, D_MODEL), BF16)],
        compiler_params=pltpu.CompilerParams(vmem_limit_bytes=VMEM_LIMIT),
        name="out_proj",
    )(x, mix_c, mix_l, mod, gain, w_out)


def _take_top16(s):
    n_rows = s.shape[0]
    ridx = _iota(s.shape, 0)
    rank = jnp.full(s.shape, PEER_TOPK, jnp.int32)
    vals = []
    for a in range(PEER_TOPK):
        mx = jnp.max(s, axis=0, keepdims=True)
        idx = jnp.min(jnp.where(s == mx, ridx, n_rows), axis=0, keepdims=True)
        hit = ridx == idx
        rank = jnp.where(hit, a, rank)
        s = jnp.where(hit, -jnp.inf, s)
        vals.append(mx)
    return vals, rank


_CAND_WIDTH = (16, 8, 5, 4, 3, 2, 2, 2)


def _sort16_pairs():
    n, pairs, p = 16, [], 1
    while p < n:
        k = p
        while k >= 1:
            for j in range(k % p, n - k, 2 * k):
                for i in range(min(k, n - j - k)):
                    if (i + j) // (2 * p) == (i + j + k) // (2 * p):
                        pairs.append((i + j, i + j + k))
            k //= 2
        p *= 2
    return pairs


_SORT16 = _sort16_pairs()
SUBLANES = 8


def _sublane_sum(x):
    for d in (4, 2, 1):
        x = x + pltpu.roll(x, d, axis=0)
    return x


def _top16_values(groups):
    g = list(groups)

    def exchange(i, j):
        if g[j] is None:
            return
        if g[i] is None:
            g[i], g[j] = g[j], None
            return
        g[i], g[j] = jnp.maximum(g[i], g[j]), jnp.minimum(g[i], g[j])

    for i, j in _SORT16:
        exchange(i, j)
    for d in (4, 2, 1):
        merged = []
        for v in range(16):
            other = g[15 - v]
            if other is None:
                merged.append(g[v])
            elif g[v] is None:
                merged.append(pltpu.roll(other, d, axis=0))
            else:
                merged.append(jnp.maximum(g[v], pltpu.roll(other, d, axis=0)))
        g = merged
        for stride in (8, 4, 2, 1):
            for i in range(16):
                if (i & stride) == 0:
                    exchange(i, i + stride)
    return g


def _topk_fast(s1, s2, a_ref, cnt_ref, b_ref, r2_ref):
    tl = s1.shape[1]
    g1 = [s1[SUBLANES * v:SUBLANES * (v + 1)] for v in range(16)]
    g2 = [s2[SUBLANES * v:SUBLANES * (v + 1)] for v in range(16)]
    v1 = _top16_values(g1)
    v2 = _top16_values(g2)
    sub = _iota((SUBLANES, tl), 0)
    v2lo, v2hi, v1hi = v2[7], v2[15], v1[15]
    for b in range(6, -1, -1):
        v2lo = jnp.where(sub == b, v2[b], v2lo)
        v2hi = jnp.where(sub == b, v2[8 + b], v2hi)
        v1hi = jnp.where(sub == b, v1[8 + b], v1hi)
    cands = [v1[0] + v2lo, v1[0] + v2hi, v1[1] + v2lo]
    for a in range(2, 8):
        cands.append(jnp.where(sub < _CAND_WIDTH[a], v1[a] + v2lo, -jnp.inf))
    cands.append(v1hi + v2[0])
    tau = _top16_values(cands + [None] * 6)[15]
    tmax = v1[0] + v2[0]
    sel = [(c >= tau).astype(F32) for c in cands]
    cnt = [_sublane_sum(sel[0] + sel[1]), _sublane_sum(sel[2])]
    for a in range(2, 8):
        cnt.append(_sublane_sum(sel[a + 1]))
    for a in range(8, 16):
        cnt.append(((v1[a] + v2[0]) >= tau).astype(F32))
    z = sel[0] * jnp.exp(cands[0] - tmax)
    for k in range(1, len(cands)):
        z = z + sel[k] * jnp.exp(cands[k] - tmax)
    inv_z = 1.0 / _sublane_sum(z)
    total = cnt[0]
    for a in range(1, 16):
        total = total + cnt[a]
    n1 = jnp.zeros((SUBLANES, tl), F32)
    n2 = jnp.zeros((SUBLANES, tl), F32)
    for v in range(16):
        rows = slice(SUBLANES * v, SUBLANES * (v + 1))
        c1 = jnp.zeros((SUBLANES, tl), F32)
        r2 = jnp.full((SUBLANES, tl), float(PEER_TOPK), F32)
        for a in range(PEER_TOPK - 1, -1, -1):
            c1 = jnp.where(g1[v] >= v1[a], cnt[a], c1)
            r2 = jnp.where(g2[v] >= v2[a], float(a), r2)
        n1 = n1 + (g1[v] >= v1[15]).astype(F32)
        n2 = n2 + (g2[v] >= v2[15]).astype(F32)
        a_ref[rows, :] = jnp.exp(g1[v] - v1[0])
        cnt_ref[rows, :] = c1
        b_ref[rows, :] = (jnp.exp(g2[v] - v2[0]) * inv_z).astype(BF16)
        r2_ref[rows, :] = r2.astype(BF16)
    bad = (total != float(PEER_TOPK))
    bad = bad | ((_sublane_sum(n1) != float(PEER_TOPK)) & (cnt[15] > 0.0))
    bad = bad | ((_sublane_sum(n2) != float(PEER_TOPK)) & (cnt[0] >= float(PEER_TOPK)))
    return bad.astype(F32)


def _topk_body(h2_ref, wq_ref, sk_ref, a_ref, cnt_ref, b_ref, r2_ref):
    qT = _dot_nt(wq_ref[...], h2_ref[...]).astype(BF16)
    s1 = _dot(sk_ref[0], qT[0:N_KEYS])
    s2 = _dot(sk_ref[1], qT[N_KEYS:2 * N_KEYS])
    bad = _topk_fast(s1, s2, a_ref, cnt_ref, b_ref, r2_ref)

    @pl.when(jnp.max(bad) > 0.0)
    def _():
        _topk_exact(s1, s2, a_ref, cnt_ref, b_ref, r2_ref)


def _topk_exact(s1, s2, a_ref, cnt_ref, b_ref, r2_ref):
    tl = s1.shape[1]
    v1, rank1 = _take_top16(s1)
    v2, rank2 = _take_top16(s2)
    V1 = jnp.concatenate(v1, axis=0)
    V2 = jnp.concatenate(v2, axis=0)
    b8 = _iota((8, tl), 0)
    pieces = [v1[0] + V2, v1[1] + V2[0:8]]
    for a in range(2, 8):
        pieces.append(jnp.where(b8 < _CAND_WIDTH[a], v1[a] + V2[0:8], -jnp.inf))
    pieces.append(V1[8:16] + v2[0])
    cand = jnp.concatenate(pieces, axis=0)
    n_c = cand.shape[0]
    cidx = _iota(cand.shape, 0)
    work = cand
    sel = jnp.zeros(cand.shape, F32)
    for _ in range(PEER_TOPK):
        mx = jnp.max(work, axis=0, keepdims=True)
        idx = jnp.min(jnp.where(work == mx, cidx, n_c), axis=0, keepdims=True)
        hit = cidx == idx
        sel = jnp.where(hit, 1.0, sel)
        work = jnp.where(hit, -jnp.inf, work)
    tmax = v1[0] + v2[0]
    z = jnp.sum(sel * jnp.exp(jnp.where(sel > 0.0, cand, tmax) - tmax), axis=0, keepdims=True)
    cnt = [jnp.sum(sel[0:16], axis=0, keepdims=True), jnp.sum(sel[16:24], axis=0, keepdims=True)]
    for a in range(2, 8):
        cnt.append(jnp.sum(sel[8 * a + 8:8 * a + 16], axis=0, keepdims=True))
    for a in range(8, 16):
        cnt.append(sel[64 + a:65 + a])
    cnt1 = jnp.zeros(s1.shape, F32)
    for a in range(PEER_TOPK):
        cnt1 = jnp.where(rank1 == a, cnt[a], cnt1)
    a_ref[...] = jnp.exp(s1 - v1[0])
    cnt_ref[...] = cnt1
    b_ref[...] = (jnp.exp(s2 - v2[0]) * (1.0 / z)).astype(BF16)
    r2_ref[...] = rank2.astype(F32).astype(BF16)


def _peer_topk(h2, wq_t, subkeys, layer):
    tl = TL_TOPK
    outs = SDS((N_TOK // tl, PEER_HEADS, N_KEYS, tl), F32)
    outs_b = SDS((N_TOK // tl, PEER_HEADS, N_KEYS, tl), BF16)
    ospec = pl.BlockSpec((None, None, N_KEYS, tl), lambda i, h: (i, h, 0, 0))
    return pl.pallas_call(
        _topk_body,
        grid=(N_TOK // tl, PEER_HEADS),
        in_specs=[pl.BlockSpec((tl, D_MODEL), lambda i, h: (i, 0)),
                  pl.BlockSpec((None, 2 * N_KEYS, D_MODEL), lambda i, h: (layer, h, 0)),
                  pl.BlockSpec((None, None, 2, N_KEYS, N_KEYS), lambda i, h: (layer, h, 0, 0, 0))],
        out_specs=[ospec] * 4,
        out_shape=[outs, outs, outs_b, outs_b],
        compiler_params=pltpu.CompilerParams(vmem_limit_bytes=VMEM_LIMIT),
        name="peer_topk",
    )(h2, wq_t, subkeys)


def _peer_step(cur, s, h2_ref, a_ref, cnt_ref, b_ref, r2_ref, u_ref, vt_ref, acc_sc, act_sc, hw_sc):
    prev = 1 - cur
    n_blocks = pl.num_programs(1) - 2
    nt = h2_ref.shape[0]
    valid = jnp.logical_and(s >= 1, s <= n_blocks).astype(F32)
    blk = jnp.clip(s - 1, 0, n_blocks - 1)
    rows_per_step = EB_PEER // N_KEYS

    for lc in range(nt // TL_TOPK):
        tc = slice(TL_TOPK * lc, TL_TOPK * (lc + 1))
        cnt_rows = [[(cnt_ref[lc, h, pl.ds(blk * rows_per_step + ii, 1), :] * valid).astype(BF16)
                     for h in range(PEER_HEADS)] for ii in range(rows_per_step)]
        a_rows = [[a_ref[lc, h, pl.ds(blk * rows_per_step + ii, 1), :].astype(BF16)
                   for h in range(PEER_HEADS)] for ii in range(rows_per_step)]
        for j, vt_half in enumerate(vt_ref):
            dr = slice(D_MODEL // 2 * j, D_MODEL // 2 * (j + 1))
            acc_sc[dr, tc] += _dot(vt_half[...], hw_sc[cur, :, tc])
        for ii in range(rows_per_step):
            er = slice(N_KEYS * ii, N_KEYS * (ii + 1))
            gate = jnp.zeros((N_KEYS, TL_TOPK), BF16)
            for h in range(PEER_HEADS):
                gate = gate + jnp.where(r2_ref[lc, h] < cnt_rows[ii][h], b_ref[lc, h] * a_rows[ii][h],
                                        jnp.zeros((), BF16))
            act = act_sc[prev, er, tc]
            gel = 0.5 * act * (1.0 + lax.erf(act * math.sqrt(0.5)))
            hw_sc[prev, er, tc] = gel.astype(BF16) * gate
        for j, u_half in enumerate(u_ref):
            hr = slice(EB_PEER // 2 * j, EB_PEER // 2 * (j + 1))
            act_sc[cur, hr, tc] = _dot_nt(u_half[...], h2_ref[tc, :])


def _peer_body(x_ref, h2_ref, mod_ref, a_ref, cnt_ref, b_ref, r2_ref, u_lo, u_hi, vt_lo, vt_hi, o_ref,
               acc_sc, act_sc, hw_sc):
    s = pl.program_id(1)

    @pl.when(s == 0)
    def _():
        acc_sc[...] = jnp.zeros_like(acc_sc)
        act_sc[...] = jnp.zeros_like(act_sc)
        hw_sc[...] = jnp.zeros_like(hw_sc)

    _peer_step(s % 2, s, h2_ref, a_ref, cnt_ref, b_ref, r2_ref, (u_lo, u_hi), (vt_lo, vt_hi), acc_sc, act_sc, hw_sc)

    @pl.when(s == pl.num_programs(1) - 1)
    def _():
        o_ref[...] = x_ref[...] + mod_ref[5:6, :] * acc_sc[...].T


def _peer_dense(x, h2, mod, at, cntt, bt, r2t, u, vt, layer):
    nt = NT_PEER
    eb = EB_PEER
    n_blocks = N_EXPERTS // eb
    fac = pl.BlockSpec((nt // TL_TOPK, PEER_HEADS, N_KEYS, TL_TOPK), lambda i, s: (i, 0, 0, 0))
    return pl.pallas_call(
        _peer_body,
        grid=(N_TOK // nt, n_blocks + 2),
        in_specs=[pl.BlockSpec((nt, D_MODEL), lambda i, s: (i, 0)),
                  pl.BlockSpec((nt, D_MODEL), lambda i, s: (i, 0)),
                  pl.BlockSpec((None, 6, D_MODEL), lambda i, s: (_group_of_tile(i, nt), 0, 0)),
                  fac, fac, fac, fac,
                  pl.BlockSpec((None, eb // 2, D_MODEL), lambda i, s: (layer, 2 * jnp.minimum(s, n_blocks - 1), 0)),
                  pl.BlockSpec((None, eb // 2, D_MODEL), lambda i, s: (layer, 2 * jnp.minimum(s, n_blocks - 1) + 1, 0)),
                  pl.BlockSpec((None, None, D_MODEL // 2, eb), lambda i, s: (layer, jnp.maximum(s - 2, 0), 0, 0)),
                  pl.BlockSpec((None, None, D_MODEL // 2, eb), lambda i, s: (layer, jnp.maximum(s - 2, 0), 1, 0))],
        out_specs=pl.BlockSpec((nt, D_MODEL), lambda i, s: (i, 0)),
        out_shape=SDS((N_TOK, D_MODEL), F32),
        scratch_shapes=[pltpu.VMEM((D_MODEL, nt), F32), pltpu.VMEM((2, eb, nt), F32),
                        pltpu.VMEM((2, eb, nt), BF16)],
        compiler_params=pltpu.CompilerParams(vmem_limit_bytes=VMEM_LIMIT,
                                             dimension_semantics=("arbitrary", "arbitrary")),
        name="peer_dense",
    )(x, h2, mod, at, cntt, bt, r2t, u, u, vt, vt)


def _rope_tables():
    t = jnp.arange(DEC_SEQ)
    row = (t // GRID_W).astype(F32)[:, None]
    col = (t % GRID_W).astype(F32)[:, None]
    lane = jnp.arange(LANES)

    def tables(dim):
        quarter = dim // 4
        inv = ROPE_BASE ** (-jnp.arange(quarter, dtype=F32) / quarter)
        d = lane % dim
        use_col = (d // (dim // 2)) == 1
        e = d % (dim // 2)
        ang = jnp.where(use_col[None, :], col * inv[e % quarter][None, :], row * inv[e % quarter][None, :])
        sign = jnp.where(e < quarter, -1.0, 1.0).astype(F32)[None, :]
        return jnp.cos(ang), jnp.sin(ang) * sign

    cb, sb = tables(B_HD)
    cc, sc = tables(C_DH)
    return jnp.stack([cb, sb, cc, sc], axis=0)


def _pack_vectors(l, a_out_gain, b_q_gain, b_k_gain, b_sink, c_q_gain, c_k_gain, c_lambda, c_out_gain):
    pad = lambda v: jnp.pad(v, (0, LANES - v.shape[0]))
    rows = [jnp.tile(b_q_gain[l], 2), jnp.tile(b_k_gain[l], 2),
            jnp.tile(c_q_gain[l].reshape(-1), 2), jnp.tile(c_k_gain[l].reshape(-1), 2),
            a_out_gain[l, :LANES], a_out_gain[l, LANES:],
            jnp.tile(c_out_gain[l], 2), pad(b_sink[l])]
    rows += [pad(c_lambda[l, r]) for r in range(4)]
    rows += [jnp.zeros((LANES,), F32)] * (PV_ROWS - len(rows))
    return jnp.stack(rows, axis=0).astype(F32)


def kernel(x_prompt, x_sample, cache_swa_k, cache_swa_v, cache_diff_k, cache_diff_v, state_mlstm_C, state_mlstm_n,
           state_mlstm_m, c, c_ctx, w_ada, b_ada, norm_gain, w_in, b_in, a_out_gain, b_q_gain, b_k_gain, b_sink,
           c_q_gain, c_k_gain, c_lambda, c_out_gain, w_out, peer_wq, peer_subkeys, peer_u, peer_v):
    P = PAST_LEN
    x = jnp.concatenate([x_prompt.reshape(N_CTX, D_MODEL), x_sample.reshape(N_LAT, D_MODEL)], axis=0)
    cond8 = jnp.concatenate([c_ctx[None, :], c, jnp.zeros((5, D_MODEL), F32)], axis=0)
    mod_all = _ada_all(cond8, w_ada, b_ada)[:, :3].reshape(DEPTH, 3, 6, D_MODEL)

    gate_pad = LANES - N_GATES
    w_in_r = jnp.concatenate([w_in[:, :, :GATE_SRC], w_in[:, :, GATE_SRC + N_GATES:],
                              w_in[:, :, GATE_SRC:GATE_SRC + N_GATES],
                              jnp.zeros((DEPTH, D_MODEL, gate_pad), F32)], axis=-1).astype(BF16)
    b_in_r = jnp.concatenate([b_in[:, :GATE_SRC], b_in[:, GATE_SRC + N_GATES:],
                              b_in[:, GATE_SRC:GATE_SRC + N_GATES],
                              jnp.zeros((DEPTH, gate_pad), F32)], axis=-1).reshape(DEPTH, 1, Z_COLS)
    w_out_b = w_out.astype(BF16)
    wq_t = jnp.swapaxes(peer_wq, 1, 2).astype(BF16)
    subkeys_b = peer_subkeys.astype(BF16)
    u_b = peer_u.astype(BF16)
    vt_b = jnp.swapaxes(peer_v.astype(BF16).reshape(DEPTH, N_EXPERTS // EB_PEER, EB_PEER, D_MODEL), 2, 3)

    rope = _rope_tables()
    ckb = cache_swa_k.reshape(DEC_BATCH, DEPTH, P, LANES)
    cvb = cache_swa_v.reshape(DEC_BATCH, DEPTH, P, LANES)
    ckc = cache_diff_k.reshape(DEC_BATCH, DEPTH, P, 2 * LANES)
    cvc = cache_diff_v.reshape(DEC_BATCH, DEPTH, P, 2 * LANES)
    sc6 = state_mlstm_C.reshape(DEC_BATCH, DEPTH, 2, 2, 2, A_DK, A_DK)
    zero = jnp.zeros_like(sc6[..., 0, :, :])
    cbd = jnp.concatenate([jnp.concatenate([sc6[..., 0, :, :], zero], axis=-1),
                           jnp.concatenate([zero, sc6[..., 1, :, :]], axis=-1)], axis=-2)
    n0 = state_mlstm_n.reshape(DEC_BATCH, DEPTH, 4, LANES)
    sm = state_mlstm_m
    m0 = jnp.zeros((DEC_BATCH, DEPTH, 1, LANES), F32)
    m0 = m0.at[:, :, 0, 4:8].set(sm[:, :, 0]).at[:, :, 0, 12:16].set(sm[:, :, 1])

    ctx = [[] for _ in range(7)]
    for l in range(DEPTH):
        lam_init = 0.8 - 0.6 * math.exp(-0.3 * l)
        pv = _pack_vectors(l, a_out_gain, b_q_gain, b_k_gain, b_sink, c_q_gain, c_k_gain, c_lambda, c_out_gain)
        z = _in_proj(x, mod_all[l], norm_gain[l, 0:1], w_in_r, b_in_r[l], l)
        mix_c, kb, vb, kc, vc, cst, nst, mst = _mixer_ctx(z, pv, lam_init)
        mix_l = _mixer_lat(z, pv, rope, ckb, cvb, ckc, cvc, cbd, n0, m0, l, lam_init)
        x, h2 = _out_proj(x, mix_c, mix_l, mod_all[l], norm_gain[l, 1:2], w_out_b, l)
        at, cntt, bt, r2t = _peer_topk(h2, wq_t, subkeys_b, l)
        x = _peer_dense(x, h2, mod_all[l], at, cntt, bt, r2t, u_b, vt_b, l)
        c4 = jnp.stack([cst[:, :, :, :A_DK, :A_DK], cst[:, :, :, A_DK:, A_DK:]], axis=3)
        for j, v in enumerate((kb, vb, kc, vc, c4.reshape(BATCH, 2, 4, A_DK, A_DK),
                               nst.reshape(BATCH, 2, 4, A_DK),
                               jnp.stack([mst[:, 0, 4:8], mst[:, 0, 12:16]], axis=1))):
            ctx[j].append(v)

    stack = lambda j: jnp.stack(ctx[j], axis=1)
    yp = x[:N_CTX].reshape(BATCH, SEQ, D_MODEL)
    ys = x[N_CTX:].reshape(DEC_BATCH, DEC_SEQ, D_MODEL)
    return (yp, ys,
            stack(0).reshape(BATCH, DEPTH, SEQ, 2, B_HD), stack(1).reshape(BATCH, DEPTH, SEQ, 2, B_HD),
            stack(2).reshape(BATCH, DEPTH, SEQ, 4, 2 * C_DH), stack(3).reshape(BATCH, DEPTH, SEQ, 4, 64),
            stack(4), stack(5), stack(6))
```
